```python
import jax, jax.numpy as jnp
from jax import lax
import numpy as np

D_MODEL = 1024
BATCH = 4
SEQ = 8192
DEPTH = 2
DEC_BATCH = 1
DEC_SEQ = 16384
PAST_LEN = 128

N_MIXERS = 2
GRID_W = 64
EPS = 1e-6
M_HEADS = 4
M_DQK = 128
M_DV = D_MODEL // M_HEADS
M_CHUNK = 128
M_SPLITS = (M_HEADS * M_DQK, 2 * M_HEADS * M_DQK, 2 * M_HEADS * M_DQK + M_HEADS * M_DV, 2 * M_HEADS * M_DQK + 2 * M_HEADS * M_DV)
M_IN = M_SPLITS[3] + 4 * M_HEADS
F_BIAS = 3.0
A_HEADS = 16
A_KV_HEADS = 4
A_GROUP = A_HEADS // A_KV_HEADS
A_HD = D_MODEL // A_HEADS
A_IN = (A_HEADS + 2 * A_KV_HEADS) * A_HD
Q_BLOCK = 128
ROPE_THETA = 10000.0
ROPE_HALF = A_HD // 2
D_FF = 4 * D_MODEL
N_A = (DEPTH + 1) // 2
N_B = DEPTH // 2

kernel_name = 'hybrid_mlstm_gqa_axial_encoder'


def rmsnorm(x, g):
    xf = x.astype(jnp.float32)
    y = xf * lax.rsqrt(jnp.mean(xf * xf, axis=-1, keepdims=True) + EPS)
    return (y * g.astype(jnp.float32)).astype(x.dtype)


def mlstm_scan(q, k, v, li, lf):
    B, H, L, dqk = q.shape
    dv = v.shape[-1]
    nc = L // M_CHUNK

    def chunks(a):
        return jnp.moveaxis(a.reshape((B, H, nc, M_CHUNK) + a.shape[3:]), 2, 0)

    lower = jnp.tril(jnp.ones((M_CHUNK, M_CHUNK), dtype=bool))

    def step(carry, inp):
        C, n, m = carry
        qc, kc, vc, lic, lfc = inp
        b = jnp.cumsum(lfc, axis=-1)
        dmat = jnp.where(lower, b[..., :, None] - b[..., None, :] + lic[..., None, :], -jnp.inf)
        inter = b + m[..., None]
        m_row = jnp.maximum(inter, jnp.max(dmat, axis=-1))
        w_inter = jnp.exp(inter - m_row)
        p = jnp.exp(dmat - m_row[..., None]) * jnp.einsum('bhjd,bhsd->bhjs', qc, kc)
        num = w_inter[..., None] * jnp.einsum('bhjd,bhde->bhje', qc, C) + jnp.einsum('bhjs,bhse->bhje', p, vc)
        den = w_inter * jnp.einsum('bhjd,bhd->bhj', qc, n) + jnp.sum(p, axis=-1)
        h = num / jnp.maximum(jnp.abs(den), jnp.exp(-m_row))[..., None]
        g = b[..., -1:] - b + lic
        m_new = jnp.maximum(b[..., -1] + m, jnp.max(g, axis=-1))
        decay = jnp.exp(b[..., -1] + m - m_new)
        wk = jnp.exp(g - m_new[..., None])
        C = decay[..., None, None] * C + jnp.einsum('bhsd,bhse->bhde', kc * wk[..., None], vc)
        n = decay[..., None] * n + jnp.einsum('bhs,bhsd->bhd', wk, kc)
        return (C, n, m_new), h

    init = (jnp.zeros((B, H, dqk, dv), jnp.float32), jnp.zeros((B, H, dqk), jnp.float32),
            jnp.full((B, H), -jnp.inf, jnp.float32))
    _, hs = lax.scan(step, init, (chunks(q), chunks(k), chunks(v), chunks(li), chunks(lf)))
    return jnp.moveaxis(hs, 0, 2).reshape(B, H, L, dv)


def mlstm_mixer(h, w_in, b_gate, g_head, w_out):
    B, L, _ = h.shape
    proj = h @ w_in
    q, k, v, o, gates = jnp.split(proj, list(M_SPLITS), axis=-1)

    def heads(a, d):
        return a.reshape(B, L, M_HEADS, d).transpose(0, 2, 1, 3).astype(jnp.float32)

    q = heads(q, M_DQK) * (M_DQK ** -0.5)
    k = heads(k, M_DQK)
    v = heads(v, M_DV)
    gates = (gates.astype(jnp.float32) + b_gate.astype(jnp.float32)).reshape(B, L, 4, M_HEADS).transpose(2, 0, 3, 1)
    li_f, lf_f = gates[0], jax.nn.log_sigmoid(gates[1])
    li_b, lf_b = gates[2], jax.nn.log_sigmoid(gates[3])
    flip = lambda a: jnp.flip(a, axis=2)
    h_fwd = mlstm_scan(q, k, v, li_f, lf_f)
    h_bwd = flip(mlstm_scan(flip(q), flip(k), flip(v), flip(li_b), flip(lf_b)))
    hs = h_fwd + h_bwd
    hs = hs * lax.rsqrt(jnp.mean(hs * hs, axis=-1, keepdims=True) + EPS)
    hs = hs.transpose(0, 2, 1, 3).reshape(B, L, M_HEADS * M_DV) * g_head.astype(jnp.float32)
    y = jax.nn.sigmoid(o.astype(jnp.float32)) * hs
    return y.astype(h.dtype) @ w_out


def rope_tables(pos):
    freqs = ROPE_THETA ** (-jnp.arange(0, ROPE_HALF, 2, dtype=jnp.float32) / ROPE_HALF)
    ang = pos.astype(jnp.float32)[:, None] * freqs[None, :]
    emb = jnp.concatenate([ang, ang], axis=-1)
    return jnp.cos(emb)[:, None, :], jnp.sin(emb)[:, None, :]


def rotate_half(x):
    x1, x2 = jnp.split(x, 2, axis=-1)
    return jnp.concatenate([-x2, x1], axis=-1)


def axial_rope(x, cos_r, sin_r, cos_c, sin_c):
    xf = x.astype(jnp.float32)
    xr, xc = xf[..., :ROPE_HALF], xf[..., ROPE_HALF:]
    out = jnp.concatenate([xr * cos_r + rotate_half(xr) * sin_r,
                           xc * cos_c + rotate_half(xc) * sin_c], axis=-1)
    return out.astype(x.dtype)


def gqa_mixer(h, w_qkv, g_q, g_k, w_o):
    B, L, _ = h.shape
    rows = L // GRID_W
    row_ids = jnp.repeat(jnp.arange(rows), GRID_W)
    col_ids = jnp.arange(rows * GRID_W) % GRID_W
    cos_r, sin_r = rope_tables(row_ids)
    cos_c, sin_c = rope_tables(col_ids)
    proj = h @ w_qkv
    q, k, v = jnp.split(proj, [A_HEADS * A_HD, (A_HEADS + A_KV_HEADS) * A_HD], axis=-1)
    q = q.reshape(B, L, A_HEADS, A_HD)
    k = k.reshape(B, L, A_KV_HEADS, A_HD)
    v = v.reshape(B, L, A_KV_HEADS, A_HD)
    q = axial_rope(rmsnorm(q, g_q), cos_r, sin_r, cos_c, sin_c)
    k = axial_rope(rmsnorm(k, g_k), cos_r, sin_r, cos_c, sin_c)
    nb = L // Q_BLOCK
    qb = q.reshape(B, nb, Q_BLOCK, A_KV_HEADS, A_GROUP, A_HD).transpose(1, 0, 2, 3, 4, 5)
    scale = A_HD ** -0.5

    def attend(qblk):
        s = jnp.einsum('bqhgd,bkhd->bhgqk', qblk, k, preferred_element_type=jnp.float32) * scale
        p = jax.nn.softmax(s, axis=-1)
        return jnp.einsum('bhgqk,bkhd->bqhgd', p.astype(v.dtype), v)

    o = lax.map(attend, qb)
    o = o.transpose(1, 0, 2, 3, 4, 5).reshape(B, L, A_HEADS * A_HD)
    return o @ w_o


def squared_relu_mlp(h, w_up, w_down):
    u = jax.nn.relu(h @ w_up)
    return (u * u) @ w_down


def trunk(x, norm_mix, norm_mlp, a_w_in, a_b_gate, a_g_head, a_w_out,
          b_w_qkv, b_g_q, b_g_k, b_w_o, w_up, w_down):
    for i in range(DEPTH):
        j = i // N_MIXERS
        h = rmsnorm(x, norm_mix[i])
        if i % N_MIXERS == 0:
            x = x + mlstm_mixer(h, a_w_in[j], a_b_gate[j], a_g_head[j], a_w_out[j])
        else:
            x = x + gqa_mixer(h, b_w_qkv[j], b_g_q[j], b_g_k[j], b_w_o[j])
        x = x + squared_relu_mlp(rmsnorm(x, norm_mlp[i]), w_up[i], w_down[i])
    return x


def setup_inputs(seed: int = 0) -> dict:
    key = jax.random.key(seed)
    ks = jax.random.split(key, 14)
    f32 = jnp.float32

    def nrm(k, shape, scale):
        return scale * jax.random.normal(k, shape, f32)

    gate_offset = jnp.repeat(jnp.array([0.0, F_BIAS, 0.0, F_BIAS], f32), M_HEADS)
    return {
        'x_prompt': nrm(ks[0], (BATCH, SEQ, D_MODEL), 1.0),
        'x_sample': nrm(ks[1], (DEC_BATCH, DEC_SEQ, D_MODEL), 1.0),
        'norm_mix': 1.0 + nrm(ks[2], (DEPTH, D_MODEL), 0.02),
        'norm_mlp': 1.0 + nrm(ks[3], (DEPTH, D_MODEL), 0.02),
        'a_w_in': nrm(ks[4], (N_A, D_MODEL, M_IN), D_MODEL ** -0.5),
        'a_b_gate': gate_offset + nrm(ks[5], (N_A, 4 * M_HEADS), 0.1),
        'a_g_head': 1.0 + nrm(ks[6], (N_A, M_HEADS * M_DV), 0.02),
        'a_w_out': nrm(ks[7], (N_A, M_HEADS * M_DV, D_MODEL), (M_HEADS * M_DV) ** -0.5),
        'b_w_qkv': nrm(ks[8], (N_B, D_MODEL, A_IN), D_MODEL ** -0.5),
        'b_g_q': 1.0 + nrm(ks[9], (N_B, A_HD), 0.02),
        'b_g_k': 1.0 + nrm(ks[10], (N_B, A_HD), 0.02),
        'b_w_o': nrm(ks[11], (N_B, A_HEADS * A_HD, D_MODEL), (A_HEADS * A_HD) ** -0.5),
        'w_up': nrm(ks[12], (DEPTH, D_MODEL, D_FF), D_MODEL ** -0.5),
        'w_down': nrm(ks[13], (DEPTH, D_FF, D_MODEL), D_FF ** -0.5),
    }


def reference(x_prompt, x_sample, norm_mix, norm_mlp, a_w_in, a_b_gate, a_g_head, a_w_out,
              b_w_qkv, b_g_q, b_g_k, b_w_o, w_up, w_down):
    y_prompt = trunk(x_prompt, norm_mix, norm_mlp, a_w_in, a_b_gate, a_g_head, a_w_out,
                     b_w_qkv, b_g_q, b_g_k, b_w_o, w_up, w_down)
    y_sample = trunk(x_sample, norm_mix, norm_mlp, a_w_in, a_b_gate, a_g_head, a_w_out,
                     b_w_qkv, b_g_q, b_g_k, b_w_o, w_up, w_down)
    return (y_prompt, y_sample)
```

```python
import functools
import math

import jax
import jax.numpy as jnp
from jax import lax
from jax.experimental import pallas as pl
from jax.experimental.pallas import tpu as pltpu

F32 = jnp.float32
BF16 = jnp.bfloat16

D_MODEL = 1024
EPS = 1e-6
GRID_W = 64
M_HEADS = 4
M_DQK = 128
M_DV = 256
M_CHUNK = 128
M_QK = M_HEADS * M_DQK
M_V = M_HEADS * M_DV
A_HEADS = 16
A_KV_HEADS = 4
A_GROUP = 4
A_HD = 64
ROPE_THETA = 10000.0
ROPE_HALF = 32
D_FF = 4096

LANES = 128
VMEM_LIMIT = 56 * 1024 * 1024

NEG_INF = float("-inf")


def _cparams(sem):
    return pltpu.CompilerParams(dimension_semantics=sem, vmem_limit_bytes=VMEM_LIMIT)


def _rms_scale(x):
    return lax.rsqrt(jnp.mean(x * x, axis=-1, keepdims=True) + EPS)


def _mlstm_proj_kernel(x_ref, g_ref, wqk_ref, wv_ref, wo_ref, wg_ref, bg_ref,
                       q_ref, k_ref, v_ref, o_ref, gate_ref):
    x = x_ref[...]
    h = (x * _rms_scale(x) * g_ref[...]).astype(BF16)
    qk = jnp.dot(h, wqk_ref[...], preferred_element_type=F32)
    q_ref[...] = (qk[:, :M_QK] * (M_DQK ** -0.5)).astype(BF16)
    k_ref[...] = qk[:, M_QK:].astype(BF16)
    v_ref[...] = jnp.dot(h, wv_ref[...], preferred_element_type=F32).astype(BF16)
    o_ref[...] = jnp.dot(h, wo_ref[...], preferred_element_type=F32)
    gate_ref[...] = jnp.dot(h, wg_ref[...], preferred_element_type=F32) + bg_ref[...]


def _mlstm_proj(x2, g, wqk, wv, wo, wg, bg, tm):
    t = x2.shape[0]
    const = lambda i: (0, 0)
    row = lambda i: (i, 0)
    return pl.pallas_call(
        _mlstm_proj_kernel,
        grid=(t // tm,),
        in_specs=[
            pl.BlockSpec((tm, D_MODEL), row),
            pl.BlockSpec((1, D_MODEL), const),
            pl.BlockSpec((D_MODEL, 2 * M_QK), const),
            pl.BlockSpec((D_MODEL, M_V), const),
            pl.BlockSpec((D_MODEL, M_V), const),
            pl.BlockSpec((D_MODEL, LANES), const),
            pl.BlockSpec((1, LANES), const),
        ],
        out_specs=[
            pl.BlockSpec((tm, M_QK), row),
            pl.BlockSpec((tm, M_QK), row),
            pl.BlockSpec((tm, M_V), row),
            pl.BlockSpec((tm, M_V), row),
            pl.BlockSpec((tm, LANES), row),
        ],
        out_shape=[
            jax.ShapeDtypeStruct((t, M_QK), BF16),
            jax.ShapeDtypeStruct((t, M_QK), BF16),
            jax.ShapeDtypeStruct((t, M_V), BF16),
            jax.ShapeDtypeStruct((t, M_V), F32),
            jax.ShapeDtypeStruct((t, LANES), F32),
        ],
        compiler_params=_cparams(("parallel",)),
        name="mlstm_proj",
    )(x2, g, wqk, wv, wo, wg, bg)


def _split3_dot(a, b_bf16, dims):
    hi = a.astype(BF16)
    r1 = a - hi.astype(F32)
    mid = r1.astype(BF16)
    lo = (r1 - mid.astype(F32)).astype(BF16)
    dn = (dims, ((), ()))
    out = lax.dot_general(hi, b_bf16, dn, preferred_element_type=F32)
    out = out + lax.dot_general(mid, b_bf16, dn, preferred_element_type=F32)
    out = out + lax.dot_general(lo, b_bf16, dn, preferred_element_type=F32)
    return out


def _log_sigmoid(x):
    return jnp.minimum(x, 0.0) - jnp.log1p(jnp.exp(-jnp.abs(x)))


def _mlstm_scan_kernel(qf_ref, kf_ref, vf_ref, gf_ref, qb_ref, kb_ref, vb_ref, gb_ref,
                       hf_ref, hb_ref, c_scr, m_scr):
    c = pl.program_id(1)

    @pl.when(c == 0)
    def _():
        c_scr[...] = jnp.zeros_like(c_scr)
        m_scr[...] = jnp.full_like(m_scr, NEG_INF)

    row_i = lax.broadcasted_iota(jnp.int32, (M_CHUNK, M_CHUNK), 0)
    col_i = lax.broadcasted_iota(jnp.int32, (M_CHUNK, M_CHUNK), 1)
    lower = row_i >= col_i
    upper = row_i <= col_i
    lane_i = lax.broadcasted_iota(jnp.int32, (M_CHUNK, LANES), 1)
    ones_col = jnp.where(lane_i == 0, 1.0, 0.0).astype(BF16)

    dirs = (
        (0, qf_ref, kf_ref, vf_ref, gf_ref, hf_ref, lower),
        (1, qb_ref, kb_ref, vb_ref, gb_ref, hb_ref, upper),
    )
    for d, q_ref, k_ref, v_ref, g_ref, h_ref, mask in dirs:
        tri = jnp.where(mask, 1.0, 0.0).astype(BF16)
        gates = g_ref[...]
        ls = _log_sigmoid(gates)
        gates_t = gates.T
        ls_t = ls.T
        cum_c = _split3_dot_lhs_const(tri, ls)
        cum_row = _split3_dot(ls_t, tri, ((1,), (1,)))
        for hd in range(M_HEADS):
            li_idx = 8 * d + hd
            lf_idx = 8 * d + 4 + hd
            a_row = gates_t[li_idx:li_idx + 1, :] - cum_row[lf_idx:lf_idx + 1, :]
            b_col = cum_c[:, lf_idx:lf_idx + 1]
            b_last = jnp.sum(ls_t[lf_idx:lf_idx + 1, :], axis=-1, keepdims=True)
            a_max = jnp.max(a_row, axis=-1, keepdims=True)
            m_old = m_scr[d * M_HEADS + hd]

            q = q_ref[0, :, hd * M_DQK:(hd + 1) * M_DQK]
            k = k_ref[0, :, hd * M_DQK:(hd + 1) * M_DQK]
            v = v_ref[0, :, hd * M_DV:(hd + 1) * M_DV]
            v_ext = jnp.concatenate([v, ones_col], axis=1)
            c_old = c_scr[d * M_HEADS + hd]

            s_qk = lax.dot_general(q, k, (((1,), (1,)), ((), ())), preferred_element_type=F32)
            dmat = jnp.where(mask, b_col + a_row, NEG_INF)
            inter = b_col + m_old
            m_row = jnp.maximum(inter, jnp.max(dmat, axis=-1, keepdims=True))
            w_inter = jnp.exp(inter - m_row)
            p = (jnp.exp(dmat - m_row) * s_qk).astype(BF16)
            num = w_inter * jnp.dot(q, c_old.astype(BF16), preferred_element_type=F32)
            num = num + jnp.dot(p, v_ext, preferred_element_type=F32)
            den = num[:, M_DV:M_DV + 1]
            h = num[:, :M_DV] / jnp.maximum(jnp.abs(den), jnp.exp(-m_row))
            h_ref[0, :, hd * M_DV:(hd + 1) * M_DV] = h

            m_new = b_last + jnp.maximum(m_old, a_max)
            decay = jnp.exp(b_last + m_old - m_new)
            wk_row = jnp.exp(b_last + a_row - m_new)
            kw_t = (k.astype(F32).T * wk_row).astype(BF16)
            c_scr[d * M_HEADS + hd] = decay * c_old + jnp.dot(kw_t, v_ext, preferred_element_type=F32)
            m_scr[d * M_HEADS + hd] = m_new


def _split3_dot_lhs_const(tri_bf16, a):
    hi = a.astype(BF16)
    r1 = a - hi.astype(F32)
    mid = r1.astype(BF16)
    lo = (r1 - mid.astype(F32)).astype(BF16)
    out = jnp.dot(tri_bf16, hi, preferred_element_type=F32)
    out = out + jnp.dot(tri_bf16, mid, preferred_element_type=F32)
    out = out + jnp.dot(tri_bf16, lo, preferred_element_type=F32)
    return out


def _mlstm_scan(q, k, v, gates):
    b, l, _ = q.shape
    nc = l // M_CHUNK
    fwd = lambda bi, ci: (bi, ci, 0)
    bwd = lambda bi, ci: (bi, nc - 1 - ci, 0)
    specs = []
    for im in (fwd, bwd):
        specs += [
            pl.BlockSpec((1, M_CHUNK, M_QK), im),
            pl.BlockSpec((1, M_CHUNK, M_QK), im),
            pl.BlockSpec((1, M_CHUNK, M_V), im),
            pl.BlockSpec((None, M_CHUNK, LANES), im),
        ]
    return pl.pallas_call(
        _mlstm_scan_kernel,
        grid=(b, nc),
        in_specs=specs,
        out_specs=[pl.BlockSpec((1, M_CHUNK, M_V), fwd), pl.BlockSpec((1, M_CHUNK, M_V), bwd)],
        out_shape=[jax.ShapeDtypeStruct((b, l, M_V), F32), jax.ShapeDtypeStruct((b, l, M_V), F32)],
        scratch_shapes=[
            pltpu.VMEM((2 * M_HEADS, M_DQK, M_DV + LANES), F32),
            pltpu.VMEM((2 * M_HEADS, 1, 1), F32),
        ],
        compiler_params=_cparams(("parallel", "arbitrary")),
        name="mlstm_scan",
    )(q, k, v, gates, q, k, v, gates)


def _mlstm_out_kernel(x_ref, hf_ref, hb_ref, o_ref, gh_ref, w_ref, y_ref):
    hs = hf_ref[...] + hb_ref[...]
    parts = []
    for hd in range(M_HEADS):
        blk = hs[:, hd * M_DV:(hd + 1) * M_DV]
        parts.append(blk * _rms_scale(blk))
    hn = jnp.concatenate(parts, axis=1) * gh_ref[...]
    y = (jax.nn.sigmoid(o_ref[...]) * hn).astype(BF16)
    y_ref[...] = x_ref[...] + jnp.dot(y, w_ref[...], preferred_element_type=F32)


def _mlstm_out(x2, hf, hb, o, gh, w, tm):
    t = x2.shape[0]
    const = lambda i: (0, 0)
    row = lambda i: (i, 0)
    return pl.pallas_call(
        _mlstm_out_kernel,
        grid=(t // tm,),
        in_specs=[
            pl.BlockSpec((tm, D_MODEL), row),
            pl.BlockSpec((tm, M_V), row),
            pl.BlockSpec((tm, M_V), row),
            pl.BlockSpec((tm, M_V), row),
            pl.BlockSpec((1, M_V), const),
            pl.BlockSpec((M_V, D_MODEL), const),
        ],
        out_specs=pl.BlockSpec((tm, D_MODEL), row),
        out_shape=jax.ShapeDtypeStruct((t, D_MODEL), F32),
        compiler_params=_cparams(("parallel",)),
        name="mlstm_out",
    )(x2, hf, hb, o, gh, w)


FF_CHUNK = 1024


def _mlp_kernel(x_ref, g_ref, wup_ref, wdn_ref, y_ref):
    x = x_ref[...]
    h = (x * _rms_scale(x) * g_ref[...]).astype(BF16)
    acc = x
    for f in range(D_FF // FF_CHUNK):
        u = jnp.dot(h, wup_ref[:, f * FF_CHUNK:(f + 1) * FF_CHUNK], preferred_element_type=F32)
        u = jnp.maximum(u, 0.0)
        u = (u * u).astype(BF16)
        acc = acc + jnp.dot(u, wdn_ref[f * FF_CHUNK:(f + 1) * FF_CHUNK, :], preferred_element_type=F32)
    y_ref[...] = acc


def _mlp(x2, g, wup, wdn, tm):
    t = x2.shape[0]
    const = lambda i: (0, 0)
    row = lambda i: (i, 0)
    return pl.pallas_call(
        _mlp_kernel,
        grid=(t // tm,),
        in_specs=[
            pl.BlockSpec((tm, D_MODEL), row),
            pl.BlockSpec((1, D_MODEL), const),
            pl.BlockSpec((D_MODEL, D_FF), const, pipeline_mode=pl.Buffered(1)),
            pl.BlockSpec((D_FF, D_MODEL), const, pipeline_mode=pl.Buffered(1)),
        ],
        out_specs=pl.BlockSpec((tm, D_MODEL), row),
        out_shape=jax.ShapeDtypeStruct((t, D_MODEL), F32),
        compiler_params=_cparams(("parallel",)),
        name="mlp",
    )(x2, g, wup, wdn)


A_Q = A_HEADS * A_HD
A_KV2 = A_KV_HEADS * LANES


def _headnorm_rope(x, gmat, gain, cos, sin_a, sin_b):
    sq = x * x
    sq_hi = sq.astype(BF16)
    sq_lo = (sq - sq_hi.astype(F32)).astype(BF16)
    ms = jnp.dot(sq_hi, gmat, preferred_element_type=F32) + jnp.dot(sq_lo, gmat, preferred_element_type=F32)
    y = x * lax.rsqrt(ms + EPS) * gain
    return y * cos + pltpu.roll(y, LANES - 16, 1) * sin_a + pltpu.roll(y, 16, 1) * sin_b


def _qkv_proj_kernel(x_ref, g_ref, wq_ref, wk_ref, wv_ref, gmat_ref, gq_ref, gk_ref,
                     cos_ref, sina_ref, sinb_ref, q_ref, qsw_ref, k_ref, v_ref):
    x = x_ref[0]
    h = (x * _rms_scale(x) * g_ref[...]).astype(BF16)
    gmat = gmat_ref[...]
    cos, sin_a, sin_b = cos_ref[...], sina_ref[...], sinb_ref[...]
    q = jnp.dot(h, wq_ref[...], preferred_element_type=F32)
    for j in range(A_Q // LANES):
        blk = _headnorm_rope(q[:, j * LANES:(j + 1) * LANES], gmat, gq_ref[...], cos, sin_a, sin_b)
        q_ref[0, :, j * LANES:(j + 1) * LANES] = blk.astype(BF16)
        qsw_ref[0, :, j * LANES:(j + 1) * LANES] = pltpu.roll(blk, A_HD, 1).astype(BF16)
    k = jnp.dot(h, wk_ref[...], preferred_element_type=F32)
    for j in range(A_KV2 // LANES):
        blk = _headnorm_rope(k[:, j * LANES:(j + 1) * LANES], gmat, gk_ref[...], cos, sin_a, sin_b)
        k_ref[0, :, j * LANES:(j + 1) * LANES] = blk.astype(BF16)
    v = jnp.dot(h, wv_ref[...], preferred_element_type=F32)
    lane = lax.broadcasted_iota(jnp.int32, v.shape, 1)
    v_ref[0] = jnp.where(lane % LANES == A_HD, 1.0, v).astype(BF16)


def _qkv_proj(x, g, wq, wk2, wv2, gmat, gq, gk, cos, sin_a, sin_b, tm):
    b, l, _ = x.shape
    const = lambda bi, i: (0, 0)
    tok = lambda bi, i: (bi, i, 0)
    pos = lambda bi, i: (i, 0)
    return pl.pallas_call(
        _qkv_proj_kernel,
        grid=(b, l // tm),
        in_specs=[
            pl.BlockSpec((1, tm, D_MODEL), tok),
            pl.BlockSpec((1, D_MODEL), const),
            pl.BlockSpec((D_MODEL, A_Q), const),
            pl.BlockSpec((D_MODEL, A_KV2), const),
            pl.BlockSpec((D_MODEL, A_KV2), const),
            pl.BlockSpec((LANES, LANES), const),
            pl.BlockSpec((1, LANES), const),
            pl.BlockSpec((1, LANES), const),
            pl.BlockSpec((tm, LANES), pos),
            pl.BlockSpec((tm, LANES), pos),
            pl.BlockSpec((tm, LANES), pos),
        ],
        out_specs=[
            pl.BlockSpec((1, tm, A_Q), tok),
            pl.BlockSpec((1, tm, A_Q), tok),
            pl.BlockSpec((1, tm, A_KV2), tok),
            pl.BlockSpec((1, tm, A_KV2), tok),
        ],
        out_shape=[
            jax.ShapeDtypeStruct((b, l, A_Q), BF16),
            jax.ShapeDtypeStruct((b, l, A_Q), BF16),
            jax.ShapeDtypeStruct((b, l, A_KV2), BF16),
            jax.ShapeDtypeStruct((b, l, A_KV2), BF16),
        ],
        compiler_params=_cparams(("parallel", "parallel")),
        name="qkv_proj",
    )(x, g, wq, wk2, wv2, gmat, gq, gk, cos, sin_a, sin_b)


def _rope_tables(l):
    pos = jnp.arange(l)
    freqs = ROPE_THETA ** (-jnp.arange(0, ROPE_HALF, 2, dtype=F32) / ROPE_HALF)
    ang_r = (pos // GRID_W).astype(F32)[:, None] * freqs[None, :]
    ang_c = (pos % GRID_W).astype(F32)[:, None] * freqs[None, :]
    emb = jnp.concatenate([ang_r, ang_r, ang_c, ang_c], axis=-1)
    cos = jnp.cos(emb)
    sin = jnp.sin(emb)
    first = (jnp.arange(A_HD) % ROPE_HALF) < (ROPE_HALF // 2)
    sin_a = jnp.where(first, -sin, 0.0)
    sin_b = jnp.where(first, 0.0, sin)
    tile2 = lambda a: jnp.concatenate([a, a], axis=-1)
    return tile2(cos), tile2(sin_a), tile2(sin_b)


def _attn_kernel(q_ref, qsw_ref, k_ref, v_ref, o_ref, m_scr, acc_scr, *, tq, tkc, nkc):
    lane_q = lax.broadcasted_iota(jnp.int32, (tq, LANES), 1)
    for hd in range(A_GROUP):
        odd = hd % 2
        pair = hd // 2
        src = qsw_ref if odd else q_ref
        q = src[0, :, pair * LANES:(pair + 1) * LANES]
        m_scr[...] = jnp.full_like(m_scr, NEG_INF)
        acc_scr[...] = jnp.zeros_like(acc_scr)

        def body(i, carry):
            start = pl.multiple_of(i * tkc, tkc)
            kk = k_ref[0, pl.ds(start, tkc), :]
            vv = v_ref[0, pl.ds(start, tkc), :]
            s = lax.dot_general(q, kk, (((1,), (1,)), ((), ())), preferred_element_type=F32)
            m_prev = m_scr[...]
            m_new = jnp.maximum(m_prev, jnp.max(s, axis=-1, keepdims=True))
            alpha = jnp.exp2(m_prev - m_new)
            p = jnp.exp2(s - m_new[:, :1]).astype(BF16)
            acc_scr[...] = alpha * acc_scr[...] + jnp.dot(p, vv, preferred_element_type=F32)
            m_scr[...] = m_new
            return carry

        lax.fori_loop(0, nkc, body, 0)
        acc = acc_scr[...]
        out = acc / acc[:, A_HD:A_HD + 1]
        if odd:
            both = jnp.where(lane_q < A_HD, even_out, pltpu.roll(out, A_HD, 1))
            o_ref[0, :, pair * LANES:(pair + 1) * LANES] = both.astype(o_ref.dtype)
        else:
            even_out = out


def _attention(q, qsw, k2, v2, tq, tkc):
    b, l, _ = q.shape
    nkc = l // tkc
    kern = functools.partial(_attn_kernel, tq=tq, tkc=tkc, nkc=nkc)
    q_map = lambda bi, g, i: (bi, i, g)
    kv_map = lambda bi, g, i: (bi, 0, g)
    return pl.pallas_call(
        kern,
        grid=(b, A_KV_HEADS, l // tq),
        in_specs=[
            pl.BlockSpec((1, tq, A_GROUP * A_HD), q_map),
            pl.BlockSpec((1, tq, A_GROUP * A_HD), q_map),
            pl.BlockSpec((1, l, LANES), kv_map),
            pl.BlockSpec((1, l, LANES), kv_map),
        ],
        out_specs=pl.BlockSpec((1, tq, A_GROUP * A_HD), q_map),
        out_shape=jax.ShapeDtypeStruct((b, l, A_Q), BF16),
        scratch_shapes=[pltpu.VMEM((tq, LANES), F32), pltpu.VMEM((tq, LANES), F32)],
        compiler_params=_cparams(("parallel", "parallel", "arbitrary")),
        name="attention",
    )(q, qsw, k2, v2)


def _attn_out_kernel(x_ref, a_ref, w_ref, y_ref):
    y_ref[...] = x_ref[...] + jnp.dot(a_ref[...], w_ref[...], preferred_element_type=F32)


def _attn_out(x2, a2, w, tm):
    t = x2.shape[0]
    const = lambda i: (0, 0)
    row = lambda i: (i, 0)
    return pl.pallas_call(
        _attn_out_kernel,
        grid=(t // tm,),
        in_specs=[
            pl.BlockSpec((tm, D_MODEL), row),
            pl.BlockSpec((tm, A_Q), row),
            pl.BlockSpec((A_Q, D_MODEL), const),
        ],
        out_specs=pl.BlockSpec((tm, D_MODEL), row),
        out_shape=jax.ShapeDtypeStruct((t, D_MODEL), F32),
        compiler_params=_cparams(("parallel",)),
        name="attn_out",
    )(x2, a2, w)


TOKEN_TILE = 512
ATTN_TQ = 256
ATTN_TKC = 512


def _prep_weights(norm_mix, norm_mlp, a_w_in, a_b_gate, a_g_head, a_w_out,
                  b_w_qkv, b_g_q, b_g_k, b_w_o, w_up, w_down):
    w_in = a_w_in[0]
    n_gate = 4 * M_HEADS
    wg = jnp.zeros((D_MODEL, LANES), F32).at[:, :n_gate].set(w_in[:, 2 * M_QK + 2 * M_V:])
    bg = jnp.zeros((1, LANES), F32).at[0, :n_gate].set(a_b_gate[0])
    w_qkv = b_w_qkv[0]
    wq = w_qkv[:, :A_Q]
    wk = w_qkv[:, A_Q:A_Q + A_KV_HEADS * A_HD].reshape(D_MODEL, A_KV_HEADS, A_HD)
    wv = w_qkv[:, A_Q + A_KV_HEADS * A_HD:].reshape(D_MODEL, A_KV_HEADS, A_HD)
    dup = lambda w: jnp.pad(w, ((0, 0), (0, 0), (0, LANES - A_HD))).reshape(D_MODEL, A_KV2)
    head_id = jnp.arange(LANES) // A_HD
    gmat = jnp.where(head_id[:, None] == head_id[None, :], 1.0 / A_HD, 0.0)
    q_gain = b_g_q[0] * (A_HD ** -0.5 * math.log2(math.e))
    return dict(
        norm_mix=norm_mix.reshape(2, 1, D_MODEL),
        norm_mlp=norm_mlp.reshape(2, 1, D_MODEL),
        wqk=w_in[:, :2 * M_QK].astype(BF16),
        wv=w_in[:, 2 * M_QK:2 * M_QK + M_V].astype(BF16),
        wo=w_in[:, 2 * M_QK + M_V:2 * M_QK + 2 * M_V].astype(BF16),
        wg=wg.astype(BF16),
        bg=bg,
        g_head=a_g_head[0].reshape(1, M_V),
        w_out=a_w_out[0].astype(BF16),
        wq=wq.astype(BF16),
        wk2=dup(wk).astype(BF16),
        wv2=dup(wv).astype(BF16),
        gmat=gmat.astype(BF16),
        gq=jnp.tile(q_gain, 2).reshape(1, LANES),
        gk=jnp.tile(b_g_k[0], 2).reshape(1, LANES),
        w_o=b_w_o[0].astype(BF16),
        w_up=w_up.astype(BF16),
        w_down=w_down.astype(BF16),
    )


def _trunk(x, w):
    b, l, _ = x.shape
    t = b * l
    tm = TOKEN_TILE
    x2 = x.reshape(t, D_MODEL)
    q, k, v, o, gates = _mlstm_proj(x2, w["norm_mix"][0], w["wqk"], w["wv"], w["wo"], w["wg"], w["bg"], tm)
    hf, hb = _mlstm_scan(q.reshape(b, l, M_QK), k.reshape(b, l, M_QK), v.reshape(b, l, M_V),
                         gates.reshape(b, l, LANES))
    x2 = _mlstm_out(x2, hf.reshape(t, M_V), hb.reshape(t, M_V), o, w["g_head"], w["w_out"], tm)
    x2 = _mlp(x2, w["norm_mlp"][0], w["w_up"][0], w["w_down"][0], tm)
    cos, sin_a, sin_b = _rope_tables(l)
    qa, qsw, k2, v2 = _qkv_proj(x2.reshape(b, l, D_MODEL), w["norm_mix"][1], w["wq"], w["wk2"], w["wv2"],
                                w["gmat"], w["gq"], w["gk"], cos, sin_a, sin_b, tm)
    att = _attention(qa, qsw, k2, v2, ATTN_TQ, ATTN_TKC)
    x2 = _attn_out(x2, att.reshape(t, A_Q), w["w_o"], tm)
    x2 = _mlp(x2, w["norm_mlp"][1], w["w_up"][1], w["w_down"][1], tm)
    return x2.reshape(b, l, D_MODEL)


def kernel(x_prompt, x_sample, norm_mix, norm_mlp, a_w_in, a_b_gate, a_g_head, a_w_out,
           b_w_qkv, b_g_q, b_g_k, b_w_o, w_up, w_down):
    w = _prep_weights(norm_mix, norm_mlp, a_w_in, a_b_gate, a_g_head, a_w_out,
                      b_w_qkv, b_g_q, b_g_k, b_w_o, w_up, w_down)
    return (_trunk(x_prompt, w), _trunk(x_sample, w))
```

```python
import functools
import math

import jax
import jax.numpy as jnp
from jax import lax
from jax.experimental import pallas as pl
from jax.experimental.pallas import tpu as pltpu

F32 = jnp.float32
BF16 = jnp.bfloat16

D_MODEL = 1024
EPS = 1e-6
GRID_W = 64
M_HEADS = 4
M_DQK = 128
M_DV = 256
M_CHUNK = 128
M_QK = M_HEADS * M_DQK
M_V = M_HEADS * M_DV
A_HEADS = 16
A_KV_HEADS = 4
A_GROUP = 4
A_HD = 64
ROPE_THETA = 10000.0
ROPE_HALF = 32
D_FF = 4096

LANES = 128
VMEM_LIMIT = 56 * 1024 * 1024

NEG_INF = float("-inf")


def _cparams(sem):
    return pltpu.CompilerParams(dimension_semantics=sem, vmem_limit_bytes=VMEM_LIMIT)


def _rms_scale(x):
    return lax.rsqrt(jnp.mean(x * x, axis=-1, keepdims=True) + EPS)


def _mlstm_proj_kernel(x_ref, g_ref, wqk_ref, wv_ref, wo_ref, wg_ref, bg_ref,
                       q_ref, k_ref, v_ref, o_ref, gate_ref):
    x = x_ref[...]
    h = (x * _rms_scale(x) * g_ref[...]).astype(BF16)
    qk = jnp.dot(h, wqk_ref[...], preferred_element_type=F32)
    q_ref[...] = (qk[:, :M_QK] * (M_DQK ** -0.5)).astype(BF16)
    k_ref[...] = qk[:, M_QK:].astype(BF16)
    v_ref[...] = jnp.dot(h, wv_ref[...], preferred_element_type=F32).astype(BF16)
    o_ref[...] = jnp.dot(h, wo_ref[...], preferred_element_type=F32)
    gate_ref[...] = jnp.dot(h, wg_ref[...], preferred_element_type=F32) + bg_ref[...]


def _mlstm_proj(x2, g, wqk, wv, wo, wg, bg, tm):
    t = x2.shape[0]
    const = lambda i: (0, 0)
    row = lambda i: (i, 0)
    return pl.pallas_call(
        _mlstm_proj_kernel,
        grid=(t // tm,),
        in_specs=[
            pl.BlockSpec((tm, D_MODEL), row),
            pl.BlockSpec((1, D_MODEL), const),
            pl.BlockSpec((D_MODEL, 2 * M_QK), const),
            pl.BlockSpec((D_MODEL, M_V), const),
            pl.BlockSpec((D_MODEL, M_V), const),
            pl.BlockSpec((D_MODEL, LANES), const),
            pl.BlockSpec((1, LANES), const),
        ],
        out_specs=[
            pl.BlockSpec((tm, M_QK), row),
            pl.BlockSpec((tm, M_QK), row),
            pl.BlockSpec((tm, M_V), row),
            pl.BlockSpec((tm, M_V), row),
            pl.BlockSpec((tm, LANES), row),
        ],
        out_shape=[
            jax.ShapeDtypeStruct((t, M_QK), BF16),
            jax.ShapeDtypeStruct((t, M_QK), BF16),
            jax.ShapeDtypeStruct((t, M_V), BF16),
            jax.ShapeDtypeStruct((t, M_V), F32),
            jax.ShapeDtypeStruct((t, LANES), F32),
        ],
        compiler_params=_cparams(("parallel",)),
        name="mlstm_proj",
    )(x2, g, wqk, wv, wo, wg, bg)


def _split3_dot(a, b_bf16, dims):
    hi = a.astype(BF16)
    r1 = a - hi.astype(F32)
    mid = r1.astype(BF16)
    lo = (r1 - mid.astype(F32)).astype(BF16)
    dn = (dims, ((), ()))
    out = lax.dot_general(hi, b_bf16, dn, preferred_element_type=F32)
    out = out + lax.dot_general(mid, b_bf16, dn, preferred_element_type=F32)
    out = out + lax.dot_general(lo, b_bf16, dn, preferred_element_type=F32)
    return out


def _log_sigmoid(x):
    return jnp.minimum(x, 0.0) - jnp.log1p(jnp.exp(-jnp.abs(x)))


def _mlstm_scan_kernel(qf_ref, kf_ref, vf_ref, gf_ref, qb_ref, kb_ref, vb_ref, gb_ref,
                       hf_ref, hb_ref, c_scr, m_scr):
    c = pl.program_id(1)

    @pl.when(c == 0)
    def _():
        c_scr[...] = jnp.zeros_like(c_scr)
        m_scr[...] = jnp.full_like(m_scr, NEG_INF)

    row_i = lax.broadcasted_iota(jnp.int32, (M_CHUNK, M_CHUNK), 0)
    col_i = lax.broadcasted_iota(jnp.int32, (M_CHUNK, M_CHUNK), 1)
    lower = row_i >= col_i
    upper = row_i <= col_i
    lane_i = lax.broadcasted_iota(jnp.int32, (M_CHUNK, LANES), 1)
    ones_col = jnp.where(lane_i == 0, 1.0, 0.0).astype(BF16)

    dirs = (
        (0, qf_ref, kf_ref, vf_ref, gf_ref, hf_ref, lower),
        (1, qb_ref, kb_ref, vb_ref, gb_ref, hb_ref, upper),
    )
    for d, q_ref, k_ref, v_ref, g_ref, h_ref, mask in dirs:
        tri = jnp.where(mask, 1.0, 0.0).astype(BF16)
        gates = g_ref[...]
        ls = _log_sigmoid(gates)
        gates_t = gates.T
        ls_t = ls.T
        cum_c = _split3_dot_lhs_const(tri, ls)
        cum_row = _split3_dot(ls_t, tri, ((1,), (1,)))
        for hd in range(M_HEADS):
            li_idx = 8 * d + hd
            lf_idx = 8 * d + 4 + hd
            a_row = gates_t[li_idx:li_idx + 1, :] - cum_row[lf_idx:lf_idx + 1, :]
            b_col = cum_c[:, lf_idx:lf_idx + 1]
            b_last = jnp.sum(ls_t[lf_idx:lf_idx + 1, :], axis=-1, keepdims=True)
            a_max = jnp.max(a_row, axis=-1, keepdims=True)
            m_old = m_scr[d * M_HEADS + hd]

            q = q_ref[0, :, hd * M_DQK:(hd + 1) * M_DQK]
            k = k_ref[0, :, hd * M_DQK:(hd + 1) * M_DQK]
            v = v_ref[0, :, hd * M_DV:(hd + 1) * M_DV]
            v_ext = jnp.concatenate([v, ones_col], axis=1)
            c_old = c_scr[d * M_HEADS + hd]

            s_qk = lax.dot_general(q, k, (((1,), (1,)), ((), ())), preferred_element_type=F32)
            dmat = jnp.where(mask, b_col + a_row, NEG_INF)
            inter = b_col + m_old
            m_row = jnp.maximum(inter, jnp.max(dmat, axis=-1, keepdims=True))
            w_inter = jnp.exp(inter - m_row)
            p = (jnp.exp(dmat - m_row) * s_qk).astype(BF16)
            num = w_inter * jnp.dot(q, c_old.astype(BF16), preferred_element_type=F32)
            num = num + jnp.dot(p, v_ext, preferred_element_type=F32)
            den = num[:, M_DV:M_DV + 1]
            h = num[:, :M_DV] / jnp.maximum(jnp.abs(den), jnp.exp(-m_row))
            h_ref[0, :, hd * M_DV:(hd + 1) * M_DV] = h

            m_new = b_last + jnp.maximum(m_old, a_max)
            decay = jnp.exp(b_last + m_old - m_new)
            wk_row = jnp.exp(b_last + a_row - m_new)
            kw_t = (k.astype(F32).T * wk_row).astype(BF16)
            c_scr[d * M_HEADS + hd] = decay * c_old + jnp.dot(kw_t, v_ext, preferred_element_type=F32)
            m_scr[d * M_HEADS + hd] = m_new


def _split3_dot_lhs_const(tri_bf16, a):
    hi = a.astype(BF16)
    r1 = a - hi.astype(F32)
    mid = r1.astype(BF16)
    lo = (r1 - mid.astype(F32)).astype(BF16)
    out = jnp.dot(tri_bf16, hi, preferred_element_type=F32)
    out = out + jnp.dot(tri_bf16, mid, preferred_element_type=F32)
    out = out + jnp.dot(tri_bf16, lo, preferred_element_type=F32)
    return out


def _mlstm_scan(q, k, v, gates):
    b, l, _ = q.shape
    nc = l // M_CHUNK
    fwd = lambda bi, ci: (bi, ci, 0)
    bwd = lambda bi, ci: (bi, nc - 1 - ci, 0)
    specs = []
    for im in (fwd, bwd):
        specs += [
            pl.BlockSpec((1, M_CHUNK, M_QK), im),
            pl.BlockSpec((1, M_CHUNK, M_QK), im),
            pl.BlockSpec((1, M_CHUNK, M_V), im),
            pl.BlockSpec((None, M_CHUNK, LANES), im),
        ]
    return pl.pallas_call(
        _mlstm_scan_kernel,
        grid=(b, nc),
        in_specs=specs,
        out_specs=[pl.BlockSpec((1, M_CHUNK, M_V), fwd), pl.BlockSpec((1, M_CHUNK, M_V), bwd)],
        out_shape=[jax.ShapeDtypeStruct((b, l, M_V), F32), jax.ShapeDtypeStruct((b, l, M_V), F32)],
        scratch_shapes=[
            pltpu.VMEM((2 * M_HEADS, M_DQK, M_DV + LANES), F32),
            pltpu.VMEM((2 * M_HEADS, 1, 1), F32),
        ],
        compiler_params=_cparams(("parallel", "arbitrary")),
        name="mlstm_scan",
    )(q, k, v, gates, q, k, v, gates)


def _mlstm_out_kernel(x_ref, hf_ref, hb_ref, o_ref, gh_ref, w_ref, y_ref):
    hs = hf_ref[...] + hb_ref[...]
    parts = []
    for hd in range(M_HEADS):
        blk = hs[:, hd * M_DV:(hd + 1) * M_DV]
        parts.append(blk * _rms_scale(blk))
    hn = jnp.concatenate(parts, axis=1) * gh_ref[...]
    y = (jax.nn.sigmoid(o_ref[...]) * hn).astype(BF16)
    y_ref[...] = x_ref[...] + jnp.dot(y, w_ref[...], preferred_element_type=F32)


def _mlstm_out(x2, hf, hb, o, gh, w, tm):
    t = x2.shape[0]
    const = lambda i: (0, 0)
    row = lambda i: (i, 0)
    return pl.pallas_call(
        _mlstm_out_kernel,
        grid=(t // tm,),
        in_specs=[
            pl.BlockSpec((tm, D_MODEL), row),
            pl.BlockSpec((tm, M_V), row),
            pl.BlockSpec((tm, M_V), row),
            pl.BlockSpec((tm, M_V), row),
            pl.BlockSpec((1, M_V), const),
            pl.BlockSpec((M_V, D_MODEL), const),
        ],
        out_specs=pl.BlockSpec((tm, D_MODEL), row),
        out_shape=jax.ShapeDtypeStruct((t, D_MODEL), F32),
        compiler_params=_cparams(("parallel",)),
        name="mlstm_out",
    )(x2, hf, hb, o, gh, w)


FF_CHUNK = 1024


def _mlp_kernel(x_ref, g_ref, wup_ref, wdn_ref, y_ref):
    x = x_ref[...]
    h = (x * _rms_scale(x) * g_ref[...]).astype(BF16)
    acc = x
    for f in range(D_FF // FF_CHUNK):
        u = jnp.dot(h, wup_ref[:, f * FF_CHUNK:(f + 1) * FF_CHUNK], preferred_element_type=F32)
        u = jnp.maximum(u, 0.0)
        u = (u * u).astype(BF16)
        acc = acc + jnp.dot(u, wdn_ref[f * FF_CHUNK:(f + 1) * FF_CHUNK, :], preferred_element_type=F32)
    y_ref[...] = acc


def _mlp(x2, g, wup, wdn, tm):
    t = x2.shape[0]
    const = lambda i: (0, 0)
    row = lambda i: (i, 0)
    return pl.pallas_call(
        _mlp_kernel,
        grid=(t // tm,),
        in_specs=[
            pl.BlockSpec((tm, D_MODEL), row),
            pl.BlockSpec((1, D_MODEL), const),
            pl.BlockSpec((D_MODEL, D_FF), const, pipeline_mode=pl.Buffered(1)),
            pl.BlockSpec((D_FF, D_MODEL), const, pipeline_mode=pl.Buffered(1)),
        ],
        out_specs=pl.BlockSpec((tm, D_MODEL), row),
        out_shape=jax.ShapeDtypeStruct((t, D_MODEL), F32),
        compiler_params=_cparams(("parallel",)),
        name="mlp",
    )(x2, g, wup, wdn)


A_Q = A_HEADS * A_HD
A_KV2 = A_KV_HEADS * LANES


def _headnorm_rope(x, gmat, gain, cos, sin_a, sin_b):
    sq = x * x
    sq_hi = sq.astype(BF16)
    sq_lo = (sq - sq_hi.astype(F32)).astype(BF16)
    ms = jnp.dot(sq_hi, gmat, preferred_element_type=F32) + jnp.dot(sq_lo, gmat, preferred_element_type=F32)
    y = x * lax.rsqrt(ms + EPS) * gain
    return y * cos + pltpu.roll(y, LANES - 16, 1) * sin_a + pltpu.roll(y, 16, 1) * sin_b


def _qkv_proj_kernel(x_ref, g_ref, wq_ref, wk_ref, wv_ref, gmat_ref, gq_ref, gk_ref,
                     cos_ref, sina_ref, sinb_ref, q_ref, k_ref, v_ref):
    x = x_ref[0]
    h = (x * _rms_scale(x) * g_ref[...]).astype(BF16)
    gmat = gmat_ref[...]
    cos, sin_a, sin_b = cos_ref[...], sina_ref[...], sinb_ref[...]
    q = jnp.dot(h, wq_ref[...], preferred_element_type=F32)
    for j in range(A_Q // LANES):
        blk = _headnorm_rope(q[:, j * LANES:(j + 1) * LANES], gmat, gq_ref[...], cos, sin_a, sin_b)
        q_ref[0, 2 * j] = blk.astype(BF16)
        q_ref[0, 2 * j + 1] = pltpu.roll(blk, A_HD, 1).astype(BF16)
    k = jnp.dot(h, wk_ref[...], preferred_element_type=F32)
    for j in range(A_KV2 // LANES):
        blk = _headnorm_rope(k[:, j * LANES:(j + 1) * LANES], gmat, gk_ref[...], cos, sin_a, sin_b)
        k_ref[0, :, j * LANES:(j + 1) * LANES] = blk.astype(BF16)
    v = jnp.dot(h, wv_ref[...], preferred_element_type=F32)
    lane = lax.broadcasted_iota(jnp.int32, v.shape, 1)
    v_ref[0] = jnp.where(lane % LANES == A_HD, 1.0, v).astype(BF16)


def _qkv_proj(x, g, wq, wk2, wv2, gmat, gq, gk, cos, sin_a, sin_b, tm):
    b, l, _ = x.shape
    const = lambda bi, i: (0, 0)
    tok = lambda bi, i: (bi, i, 0)
    pos = lambda bi, i: (i, 0)
    return pl.pallas_call(
        _qkv_proj_kernel,
        grid=(b, l // tm),
        in_specs=[
            pl.BlockSpec((1, tm, D_MODEL), tok),
            pl.BlockSpec((1, D_MODEL), const),
            pl.BlockSpec((D_MODEL, A_Q), const),
            pl.BlockSpec((D_MODEL, A_KV2), const),
            pl.BlockSpec((D_MODEL, A_KV2), const),
            pl.BlockSpec((LANES, LANES), const),
            pl.BlockSpec((1, LANES), const),
            pl.BlockSpec((1, LANES), const),
            pl.BlockSpec((tm, LANES), pos),
            pl.BlockSpec((tm, LANES), pos),
            pl.BlockSpec((tm, LANES), pos),
        ],
        out_specs=[
            pl.BlockSpec((1, A_HEADS, tm, LANES), lambda bi, i: (bi, 0, i, 0)),
            pl.BlockSpec((1, tm, A_KV2), tok),
            pl.BlockSpec((1, tm, A_KV2), tok),
        ],
        out_shape=[
            jax.ShapeDtypeStruct((b, A_HEADS, l, LANES), BF16),
            jax.ShapeDtypeStruct((b, l, A_KV2), BF16),
            jax.ShapeDtypeStruct((b, l, A_KV2), BF16),
        ],
        compiler_params=_cparams(("parallel", "parallel")),
        name="qkv_proj",
    )(x, g, wq, wk2, wv2, gmat, gq, gk, cos, sin_a, sin_b)


def _rope_tables(l):
    pos = jnp.arange(l)
    freqs = ROPE_THETA ** (-jnp.arange(0, ROPE_HALF, 2, dtype=F32) / ROPE_HALF)
    ang_r = (pos // GRID_W).astype(F32)[:, None] * freqs[None, :]
    ang_c = (pos % GRID_W).astype(F32)[:, None] * freqs[None, :]
    emb = jnp.concatenate([ang_r, ang_r, ang_c, ang_c], axis=-1)
    cos = jnp.cos(emb)
    sin = jnp.sin(emb)
    first = (jnp.arange(A_HD) % ROPE_HALF) < (ROPE_HALF // 2)
    sin_a = jnp.where(first, -sin, 0.0)
    sin_b = jnp.where(first, 0.0, sin)
    tile2 = lambda a: jnp.concatenate([a, a], axis=-1)
    return tile2(cos), tile2(sin_a), tile2(sin_b)


def _attn_kernel(q_ref, k_ref, v_ref, o_ref, m_scr, acc_scr, *, tq, tkc, nkc):
    q = q_ref[0].reshape(A_GROUP * tq, LANES)
    m_scr[...] = jnp.full_like(m_scr, NEG_INF)
    acc_scr[...] = jnp.zeros_like(acc_scr)

    def body(i, carry):
        start = pl.multiple_of(i * tkc, tkc)
        kk = k_ref[0, pl.ds(start, tkc), :]
        vv = v_ref[0, pl.ds(start, tkc), :]
        s = lax.dot_general(q, kk, (((1,), (1,)), ((), ())), preferred_element_type=F32)
        blocks = [s[:, j * LANES:(j + 1) * LANES] for j in range(tkc // LANES)]
        blk_max = functools.reduce(jnp.maximum, blocks)
        m_prev = m_scr[...]
        m_new = jnp.maximum(m_prev, jnp.max(blk_max, axis=-1, keepdims=True))
        alpha = jnp.exp2(m_prev - m_new)
        p = jnp.concatenate([jnp.exp2(blk - m_new).astype(BF16) for blk in blocks], axis=1)
        acc_scr[...] = alpha * acc_scr[...] + jnp.dot(p, vv, preferred_element_type=F32)
        m_scr[...] = m_new
        return carry

    lax.fori_loop(0, nkc, body, 0)
    acc = acc_scr[...]
    out = acc / acc[:, A_HD:A_HD + 1]
    lane_q = lax.broadcasted_iota(jnp.int32, (tq, LANES), 1)
    for pair in range(A_GROUP // 2):
        even = out[(2 * pair) * tq:(2 * pair + 1) * tq]
        odd = out[(2 * pair + 1) * tq:(2 * pair + 2) * tq]
        both = jnp.where(lane_q < A_HD, even, pltpu.roll(odd, A_HD, 1))
        o_ref[0, :, pair * LANES:(pair + 1) * LANES] = both.astype(o_ref.dtype)


def _attention(q, k2, v2, tq, tkc):
    b, _, l, _ = q.shape
    nkc = l // tkc
    kern = functools.partial(_attn_kernel, tq=tq, tkc=tkc, nkc=nkc)
    kv_map = lambda bi, g, i: (bi, 0, g)
    return pl.pallas_call(
        kern,
        grid=(b, A_KV_HEADS, l // tq),
        in_specs=[
            pl.BlockSpec((1, A_GROUP, tq, LANES), lambda bi, g, i: (bi, g, i, 0)),
            pl.BlockSpec((1, l, LANES), kv_map),
            pl.BlockSpec((1, l, LANES), kv_map),
        ],
        out_specs=pl.BlockSpec((1, tq, A_GROUP * A_HD), lambda bi, g, i: (bi, i, g)),
        out_shape=jax.ShapeDtypeStruct((b, l, A_Q), BF16),
        scratch_shapes=[pltpu.VMEM((A_GROUP * tq, LANES), F32), pltpu.VMEM((A_GROUP * tq, LANES), F32)],
        compiler_params=_cparams(("parallel", "parallel", "arbitrary")),
        name="attention",
    )(q, k2, v2)


def _attn_out_kernel(x_ref, a_ref, w_ref, y_ref):
    y_ref[...] = x_ref[...] + jnp.dot(a_ref[...], w_ref[...], preferred_element_type=F32)


def _attn_out(x2, a2, w, tm):
    t = x2.shape[0]
    const = lambda i: (0, 0)
    row = lambda i: (i, 0)
    return pl.pallas_call(
        _attn_out_kernel,
        grid=(t // tm,),
        in_specs=[
            pl.BlockSpec((tm, D_MODEL), row),
            pl.BlockSpec((tm, A_Q), row),
            pl.BlockSpec((A_Q, D_MODEL), const),
        ],
        out_specs=pl.BlockSpec((tm, D_MODEL), row),
        out_shape=jax.ShapeDtypeStruct((t, D_MODEL), F32),
        compiler_params=_cparams(("parallel",)),
        name="attn_out",
    )(x2, a2, w)


TOKEN_TILE = 512
ATTN_TQ = 256
ATTN_TKC = 512


def _prep_weights(norm_mix, norm_mlp, a_w_in, a_b_gate, a_g_head, a_w_out,
                  b_w_qkv, b_g_q, b_g_k, b_w_o, w_up, w_down):
    w_in = a_w_in[0]
    n_gate = 4 * M_HEADS
    wg = jnp.zeros((D_MODEL, LANES), F32).at[:, :n_gate].set(w_in[:, 2 * M_QK + 2 * M_V:])
    bg = jnp.zeros((1, LANES), F32).at[0, :n_gate].set(a_b_gate[0])
    w_qkv = b_w_qkv[0]
    wq = w_qkv[:, :A_Q]
    wk = w_qkv[:, A_Q:A_Q + A_KV_HEADS * A_HD].reshape(D_MODEL, A_KV_HEADS, A_HD)
    wv = w_qkv[:, A_Q + A_KV_HEADS * A_HD:].reshape(D_MODEL, A_KV_HEADS, A_HD)
    dup = lambda w: jnp.pad(w, ((0, 0), (0, 0), (0, LANES - A_HD))).reshape(D_MODEL, A_KV2)
    head_id = jnp.arange(LANES) // A_HD
    gmat = jnp.where(head_id[:, None] == head_id[None, :], 1.0 / A_HD, 0.0)
    q_gain = b_g_q[0] * (A_HD ** -0.5 * math.log2(math.e))
    return dict(
        norm_mix=norm_mix.reshape(2, 1, D_MODEL),
        norm_mlp=norm_mlp.reshape(2, 1, D_MODEL),
        wqk=w_in[:, :2 * M_QK].astype(BF16),
        wv=w_in[:, 2 * M_QK:2 * M_QK + M_V].astype(BF16),
        wo=w_in[:, 2 * M_QK + M_V:2 * M_QK + 2 * M_V].astype(BF16),
        wg=wg.astype(BF16),
        bg=bg,
        g_head=a_g_head[0].reshape(1, M_V),
        w_out=a_w_out[0].astype(BF16),
        wq=wq.astype(BF16),
        wk2=dup(wk).astype(BF16),
        wv2=dup(wv).astype(BF16),
        gmat=gmat.astype(BF16),
        gq=jnp.tile(q_gain, 2).reshape(1, LANES),
        gk=jnp.tile(b_g_k[0], 2).reshape(1, LANES),
        w_o=b_w_o[0].astype(BF16),
        w_up=w_up.astype(BF16),
        w_down=w_down.astype(BF16),
    )


def _trunk(x, w):
    b, l, _ = x.shape
    t = b * l
    tm = TOKEN_TILE
    x2 = x.reshape(t, D_MODEL)
    q, k, v, o, gates = _mlstm_proj(x2, w["norm_mix"][0], w["wqk"], w["wv"], w["wo"], w["wg"], w["bg"], tm)
    hf, hb = _mlstm_scan(q.reshape(b, l, M_QK), k.reshape(b, l, M_QK), v.reshape(b, l, M_V),
                         gates.reshape(b, l, LANES))
    x2 = _mlstm_out(x2, hf.reshape(t, M_V), hb.reshape(t, M_V), o, w["g_head"], w["w_out"], tm)
    x2 = _mlp(x2, w["norm_mlp"][0], w["w_up"][0], w["w_down"][0], tm)
    cos, sin_a, sin_b = _rope_tables(l)
    qa, k2, v2 = _qkv_proj(x2.reshape(b, l, D_MODEL), w["norm_mix"][1], w["wq"], w["wk2"], w["wv2"],
                           w["gmat"], w["gq"], w["gk"], cos, sin_a, sin_b, tm)
    att = _attention(qa, k2, v2, ATTN_TQ, ATTN_TKC)
    x2 = _attn_out(x2, att.reshape(t, A_Q), w["w_o"], tm)
    x2 = _mlp(x2, w["norm_mlp"][1], w["w_up"][1], w["w_down"][1], tm)
    return x2.reshape(b, l, D_MODEL)


def kernel(x_prompt, x_sample, norm_mix, norm_mlp, a_w_in, a_b_gate, a_g_head, a_w_out,
           b_w_qkv, b_g_q, b_g_k, b_w_o, w_up, w_down):
    w = _prep_weights(norm_mix, norm_mlp, a_w_in, a_b_gate, a_g_head, a_w_out,
                      b_w_qkv, b_g_q, b_g_k, b_w_o, w_up, w_down)
    return (_trunk(x_prompt, w), _trunk(x_sample, w))
```

```python
import functools
import math

import jax
import jax.numpy as jnp
from jax import lax
from jax.experimental import pallas as pl
from jax.experimental.pallas import tpu as pltpu

F32 = jnp.float32
BF16 = jnp.bfloat16

D_MODEL = 1024
EPS = 1e-6
GRID_W = 64
M_HEADS = 4
M_DQK = 128
M_DV = 256
M_CHUNK = 128
M_QK = M_HEADS * M_DQK
M_V = M_HEADS * M_DV
A_HEADS = 16
A_KV_HEADS = 4
A_GROUP = 4
A_HD = 64
ROPE_THETA = 10000.0
ROPE_HALF = 32
D_FF = 4096

LANES = 128
VMEM_LIMIT = 56 * 1024 * 1024

NEG_INF = float("-inf")


def _cparams(sem):
    return pltpu.CompilerParams(dimension_semantics=sem, vmem_limit_bytes=VMEM_LIMIT)


def _rms_scale(x):
    return lax.rsqrt(jnp.mean(x * x, axis=-1, keepdims=True) + EPS)


def _mlstm_proj_kernel(x_ref, g_ref, wqk_ref, wv_ref, wo_ref, wg_ref, bg_ref,
                       q_ref, k_ref, v_ref, o_ref, gate_ref):
    x = x_ref[...]
    h = (x * _rms_scale(x) * g_ref[...]).astype(BF16)
    qk = jnp.dot(h, wqk_ref[...], preferred_element_type=F32)
    q_ref[...] = (qk[:, :M_QK] * (M_DQK ** -0.5)).astype(BF16)
    k_ref[...] = qk[:, M_QK:].astype(BF16)
    v_ref[...] = jnp.dot(h, wv_ref[...], preferred_element_type=F32).astype(BF16)
    o_ref[...] = jnp.dot(h, wo_ref[...], preferred_element_type=F32)
    gate_ref[...] = jnp.dot(h, wg_ref[...], preferred_element_type=F32) + bg_ref[...]


def _mlstm_proj(x2, g, wqk, wv, wo, wg, bg, tm):
    t = x2.shape[0]
    const = lambda i: (0, 0)
    row = lambda i: (i, 0)
    return pl.pallas_call(
        _mlstm_proj_kernel,
        grid=(t // tm,),
        in_specs=[
            pl.BlockSpec((tm, D_MODEL), row),
            pl.BlockSpec((1, D_MODEL), const),
            pl.BlockSpec((D_MODEL, 2 * M_QK), const),
            pl.BlockSpec((D_MODEL, M_V), const),
            pl.BlockSpec((D_MODEL, M_V), const),
            pl.BlockSpec((D_MODEL, LANES), const),
            pl.BlockSpec((1, LANES), const),
        ],
        out_specs=[
            pl.BlockSpec((tm, M_QK), row),
            pl.BlockSpec((tm, M_QK), row),
            pl.BlockSpec((tm, M_V), row),
            pl.BlockSpec((tm, M_V), row),
            pl.BlockSpec((tm, LANES), row),
        ],
        out_shape=[
            jax.ShapeDtypeStruct((t, M_QK), BF16),
            jax.ShapeDtypeStruct((t, M_QK), BF16),
            jax.ShapeDtypeStruct((t, M_V), BF16),
            jax.ShapeDtypeStruct((t, M_V), F32),
            jax.ShapeDtypeStruct((t, LANES), F32),
        ],
        compiler_params=_cparams(("parallel",)),
        name="mlstm_proj",
    )(x2, g, wqk, wv, wo, wg, bg)


def _split3_dot(a, b_bf16, dims):
    hi = a.astype(BF16)
    r1 = a - hi.astype(F32)
    mid = r1.astype(BF16)
    lo = (r1 - mid.astype(F32)).astype(BF16)
    dn = (dims, ((), ()))
    out = lax.dot_general(hi, b_bf16, dn, preferred_element_type=F32)
    out = out + lax.dot_general(mid, b_bf16, dn, preferred_element_type=F32)
    out = out + lax.dot_general(lo, b_bf16, dn, preferred_element_type=F32)
    return out


N_GATE_ROWS = 4 * M_HEADS


def _log_sigmoid(x):
    return jnp.minimum(x, 0.0) - jnp.log1p(jnp.exp(-jnp.abs(x)))


def _mlstm_scan_kernel(qf_ref, kf_ref, vf_ref, gf_ref, qb_ref, kb_ref, vb_ref, gb_ref,
                       hf_ref, hb_ref, c_scr, m_scr):
    c = pl.program_id(1)

    @pl.when(c == 0)
    def _():
        c_scr[...] = jnp.zeros_like(c_scr)
        m_scr[...] = jnp.full_like(m_scr, NEG_INF)

    row_i = lax.broadcasted_iota(jnp.int32, (M_CHUNK, M_CHUNK), 0)
    col_i = lax.broadcasted_iota(jnp.int32, (M_CHUNK, M_CHUNK), 1)
    lower = row_i >= col_i
    upper = row_i <= col_i
    lane_i = lax.broadcasted_iota(jnp.int32, (M_CHUNK, LANES), 1)
    ones_col = jnp.where(lane_i == 0, 1.0, 0.0).astype(BF16)

    dirs = (
        (0, qf_ref, kf_ref, vf_ref, gf_ref, hf_ref, lower),
        (1, qb_ref, kb_ref, vb_ref, gb_ref, hb_ref, upper),
    )
    units = [dict(d=d, hd=hd, u=d * M_HEADS + hd, q_ref=t[1], k_ref=t[2], v_ref=t[3], h_ref=t[5], mask=t[6])
             for d, t in enumerate(dirs) for hd in range(M_HEADS)]

    for d, _, _, _, g_ref, _, mask in dirs:
        tri = jnp.where(mask, 1.0, 0.0).astype(BF16)
        gates = g_ref[...]
        ls = _log_sigmoid(gates)
        gates_t = gates.T[:N_GATE_ROWS]
        ls_t = ls.T[:N_GATE_ROWS]
        cum_row = _split3_dot(ls_t, tri, ((1,), (1,)))
        cum_c = jnp.concatenate([cum_row, jnp.zeros((M_CHUNK - N_GATE_ROWS, M_CHUNK), F32)], axis=0).T
        for un in units[d * M_HEADS:(d + 1) * M_HEADS]:
            li_idx = 8 * d + un["hd"]
            lf_idx = li_idx + M_HEADS
            un["a_row"] = gates_t[li_idx:li_idx + 1, :] - cum_row[lf_idx:lf_idx + 1, :]
            un["b_col"] = cum_c[:, lf_idx:lf_idx + 1]
            un["b_last"] = jnp.sum(ls_t[lf_idx:lf_idx + 1, :], axis=-1, keepdims=True)

        for un in units[d * M_HEADS:(d + 1) * M_HEADS]:
            hd = un["hd"]
            q = un["q_ref"][0, :, hd * M_DQK:(hd + 1) * M_DQK]
            k = un["k_ref"][0, :, hd * M_DQK:(hd + 1) * M_DQK]
            v = un["v_ref"][0, :, hd * M_DV:(hd + 1) * M_DV]
            v_ext = jnp.concatenate([v, ones_col], axis=1)
            c_old = c_scr[un["u"]]
            m_old = m_scr[un["u"]]
            un["v_ext"], un["m_old"] = v_ext, m_old
            un["s_qk"] = lax.dot_general(q, k, (((1,), (1,)), ((), ())), preferred_element_type=F32)
            un["q_c"] = jnp.dot(q, c_old.astype(BF16), preferred_element_type=F32)

            a_row, b_last = un["a_row"], un["b_last"]
            m_new = b_last + jnp.maximum(m_old, jnp.max(a_row, axis=-1, keepdims=True))
            decay = jnp.exp(b_last + m_old - m_new)
            wk_row = jnp.exp(b_last + a_row - m_new)
            kw_t = (k.astype(F32).T * wk_row).astype(BF16)
            c_scr[un["u"]] = decay * c_old + jnp.dot(kw_t, v_ext, preferred_element_type=F32)
            m_scr[un["u"]] = m_new

    for un in units:
        dmat = jnp.where(un["mask"], un["b_col"] + un["a_row"], NEG_INF)
        inter = un["b_col"] + un["m_old"]
        m_row = jnp.maximum(inter, jnp.max(dmat, axis=-1, keepdims=True))
        un["m_row"] = m_row
        un["w_inter"] = jnp.exp(inter - m_row)
        un["p"] = (jnp.exp(dmat - m_row) * un["s_qk"]).astype(BF16)

    for un in units:
        hd = un["hd"]
        num = un["w_inter"] * un["q_c"] + jnp.dot(un["p"], un["v_ext"], preferred_element_type=F32)
        den = num[:, M_DV:M_DV + 1]
        h = num[:, :M_DV] / jnp.maximum(jnp.abs(den), jnp.exp(-un["m_row"]))
        un["h_ref"][0, :, hd * M_DV:(hd + 1) * M_DV] = h


def _mlstm_scan(q, k, v, gates):
    b, l, _ = q.shape
    nc = l // M_CHUNK
    fwd = lambda bi, ci: (bi, ci, 0)
    bwd = lambda bi, ci: (bi, nc - 1 - ci, 0)
    specs = []
    for im in (fwd, bwd):
        specs += [
            pl.BlockSpec((1, M_CHUNK, M_QK), im),
            pl.BlockSpec((1, M_CHUNK, M_QK), im),
            pl.BlockSpec((1, M_CHUNK, M_V), im),
            pl.BlockSpec((None, M_CHUNK, LANES), im),
        ]
    return pl.pallas_call(
        _mlstm_scan_kernel,
        grid=(b, nc),
        in_specs=specs,
        out_specs=[pl.BlockSpec((1, M_CHUNK, M_V), fwd), pl.BlockSpec((1, M_CHUNK, M_V), bwd)],
        out_shape=[jax.ShapeDtypeStruct((b, l, M_V), F32), jax.ShapeDtypeStruct((b, l, M_V), F32)],
        scratch_shapes=[
            pltpu.VMEM((2 * M_HEADS, M_DQK, M_DV + LANES), F32),
            pltpu.VMEM((2 * M_HEADS, 1, 1), F32),
        ],
        compiler_params=_cparams(("parallel", "arbitrary")),
        name="mlstm_scan",
    )(q, k, v, gates, q, k, v, gates)


FF_CHUNK = 1024


def _mlp_residual(x, g_ref, wup_ref, wdn_ref):
    h = (x * _rms_scale(x) * g_ref[...]).astype(BF16)
    acc = x
    for f in range(D_FF // FF_CHUNK):
        u = jnp.dot(h, wup_ref[:, f * FF_CHUNK:(f + 1) * FF_CHUNK], preferred_element_type=F32)
        u = jnp.maximum(u, 0.0)
        u = (u * u).astype(BF16)
        acc = acc + jnp.dot(u, wdn_ref[f * FF_CHUNK:(f + 1) * FF_CHUNK, :], preferred_element_type=F32)
    return acc


def _mlstm_out_mlp_kernel(x_ref, hf_ref, hb_ref, o_ref, gh_ref, w_ref, g_ref, wup_ref, wdn_ref, y_ref):
    hs = hf_ref[...] + hb_ref[...]
    parts = []
    for hd in range(M_HEADS):
        blk = hs[:, hd * M_DV:(hd + 1) * M_DV]
        parts.append(blk * _rms_scale(blk))
    hn = jnp.concatenate(parts, axis=1) * gh_ref[...]
    y = (jax.nn.sigmoid(o_ref[...]) * hn).astype(BF16)
    x1 = x_ref[...] + jnp.dot(y, w_ref[...], preferred_element_type=F32)
    y_ref[...] = _mlp_residual(x1, g_ref, wup_ref, wdn_ref)


def _attn_out_mlp_kernel(x_ref, a_ref, w_ref, g_ref, wup_ref, wdn_ref, y_ref):
    x1 = x_ref[...] + jnp.dot(a_ref[...], w_ref[...], preferred_element_type=F32)
    y_ref[...] = _mlp_residual(x1, g_ref, wup_ref, wdn_ref)


def _out_mlp(body, name, x2, acts, consts, w_out, g, wup, wdn, tm):
    t = x2.shape[0]
    const = lambda i: (0, 0)
    row = lambda i: (i, 0)
    once = dict(pipeline_mode=pl.Buffered(1))
    return pl.pallas_call(
        body,
        grid=(t // tm,),
        in_specs=(
            [pl.BlockSpec((tm, D_MODEL), row)]
            + [pl.BlockSpec((tm, a.shape[1]), row) for a in acts]
            + [pl.BlockSpec(c.shape, const, **once) for c in consts]
            + [
                pl.BlockSpec(w_out.shape, const, **once),
                pl.BlockSpec((1, D_MODEL), const, **once),
                pl.BlockSpec((D_MODEL, D_FF), const, **once),
                pl.BlockSpec((D_FF, D_MODEL), const, **once),
            ]
        ),
        out_specs=pl.BlockSpec((tm, D_MODEL), row),
        out_shape=jax.ShapeDtypeStruct((t, D_MODEL), F32),
        compiler_params=_cparams(("parallel",)),
        name=name,
    )(x2, *acts, *consts, w_out, g, wup, wdn)


A_Q = A_HEADS * A_HD
A_KV2 = A_KV_HEADS * LANES


def _headnorm_rope(x, gmat, gain, cos, sin_a, sin_b):
    sq = x * x
    sq_hi = sq.astype(BF16)
    sq_lo = (sq - sq_hi.astype(F32)).astype(BF16)
    ms = jnp.dot(sq_hi, gmat, preferred_element_type=F32) + jnp.dot(sq_lo, gmat, preferred_element_type=F32)
    y = x * lax.rsqrt(ms + EPS) * gain
    return y * cos + pltpu.roll(y, LANES - 16, 1) * sin_a + pltpu.roll(y, 16, 1) * sin_b


def _qkv_proj_kernel(x_ref, g_ref, wq_ref, wk_ref, wv_ref, gmat_ref, gq_ref, gk_ref,
                     cos_ref, sina_ref, sinb_ref, q_ref, k_ref, v_ref):
    x = x_ref[0]
    h = (x * _rms_scale(x) * g_ref[...]).astype(BF16)
    gmat = gmat_ref[...]
    cos, sin_a, sin_b = cos_ref[...], sina_ref[...], sinb_ref[...]
    q = jnp.dot(h, wq_ref[...], preferred_element_type=F32)
    for j in range(A_Q // LANES):
        blk = _headnorm_rope(q[:, j * LANES:(j + 1) * LANES], gmat, gq_ref[...], cos, sin_a, sin_b)
        q_ref[0, 2 * j] = blk.astype(BF16)
        q_ref[0, 2 * j + 1] = pltpu.roll(blk, A_HD, 1).astype(BF16)
    k = jnp.dot(h, wk_ref[...], preferred_element_type=F32)
    for j in range(A_KV2 // LANES):
        blk = _headnorm_rope(k[:, j * LANES:(j + 1) * LANES], gmat, gk_ref[...], cos, sin_a, sin_b)
        k_ref[0, :, j * LANES:(j + 1) * LANES] = blk.astype(BF16)
    v = jnp.dot(h, wv_ref[...], preferred_element_type=F32)
    lane = lax.broadcasted_iota(jnp.int32, v.shape, 1)
    v_ref[0] = jnp.where(lane % LANES == A_HD, 1.0, v).astype(BF16)


def _qkv_proj(x, g, wq, wk2, wv2, gmat, gq, gk, cos, sin_a, sin_b, tm):
    b, l, _ = x.shape
    const = lambda bi, i: (0, 0)
    tok = lambda bi, i: (bi, i, 0)
    pos = lambda bi, i: (i, 0)
    return pl.pallas_call(
        _qkv_proj_kernel,
        grid=(b, l // tm),
        in_specs=[
            pl.BlockSpec((1, tm, D_MODEL), tok),
            pl.BlockSpec((1, D_MODEL), const),
            pl.BlockSpec((D_MODEL, A_Q), const),
            pl.BlockSpec((D_MODEL, A_KV2), const),
            pl.BlockSpec((D_MODEL, A_KV2), const),
            pl.BlockSpec((LANES, LANES), const),
            pl.BlockSpec((1, LANES), const),
            pl.BlockSpec((1, LANES), const),
            pl.BlockSpec((tm, LANES), pos),
            pl.BlockSpec((tm, LANES), pos),
            pl.BlockSpec((tm, LANES), pos),
        ],
        out_specs=[
            pl.BlockSpec((1, A_HEADS, tm, LANES), lambda bi, i: (bi, 0, i, 0)),
            pl.BlockSpec((1, tm, A_KV2), tok),
            pl.BlockSpec((1, tm, A_KV2), tok),
        ],
        out_shape=[
            jax.ShapeDtypeStruct((b, A_HEADS, l, LANES), BF16),
            jax.ShapeDtypeStruct((b, l, A_KV2), BF16),
            jax.ShapeDtypeStruct((b, l, A_KV2), BF16),
        ],
        compiler_params=_cparams(("parallel", "parallel")),
        name="qkv_proj",
    )(x, g, wq, wk2, wv2, gmat, gq, gk, cos, sin_a, sin_b)


def _rope_tables(l):
    pos = jnp.arange(l)
    freqs = ROPE_THETA ** (-jnp.arange(0, ROPE_HALF, 2, dtype=F32) / ROPE_HALF)
    ang_r = (pos // GRID_W).astype(F32)[:, None] * freqs[None, :]
    ang_c = (pos % GRID_W).astype(F32)[:, None] * freqs[None, :]
    emb = jnp.concatenate([ang_r, ang_r, ang_c, ang_c], axis=-1)
    cos = jnp.cos(emb)
    sin = jnp.sin(emb)
    first = (jnp.arange(A_HD) % ROPE_HALF) < (ROPE_HALF // 2)
    sin_a = jnp.where(first, -sin, 0.0)
    sin_b = jnp.where(first, 0.0, sin)
    tile2 = lambda a: jnp.concatenate([a, a], axis=-1)
    return tile2(cos), tile2(sin_a), tile2(sin_b)


def _attn_kernel(q_ref, k_ref, v_ref, o_ref, m_scr, acc_scr, s_scr, *, tq, tkc, nkc, unroll):
    q = q_ref[0].reshape(A_GROUP * tq, LANES)
    m_scr[...] = jnp.full_like(m_scr, NEG_INF)
    acc_scr[...] = jnp.zeros_like(acc_scr)

    def scores(c):
        kk = k_ref[0, pl.ds(pl.multiple_of(c * tkc, tkc), tkc), :]
        return lax.dot_general(q, kk, (((1,), (1,)), ((), ())), preferred_element_type=F32)

    def softmax_pv(s_ref, c):
        vv = v_ref[0, pl.ds(pl.multiple_of(c * tkc, tkc), tkc), :]
        blocks = [s_ref[:, j * LANES:(j + 1) * LANES] for j in range(tkc // LANES)]
        blk_max = functools.reduce(jnp.maximum, blocks)
        m_prev = m_scr[...]
        m_new = jnp.maximum(m_prev, jnp.max(blk_max, axis=-1, keepdims=True))
        alpha = jnp.exp2(m_prev - m_new)
        p = jnp.concatenate([jnp.exp2(blk - m_new).astype(BF16) for blk in blocks], axis=1)
        acc_scr[...] = alpha * acc_scr[...] + jnp.dot(p, vv, preferred_element_type=F32)
        m_scr[...] = m_new

    def group(j, last):
        for u in range(unroll):
            c = j * unroll + u
            if not (last and u == unroll - 1):
                s_scr[(u + 1) % 2] = scores(c + 1)
            softmax_pv(s_scr.at[u % 2], c)

    s_scr[0] = scores(0)
    n_groups = nkc // unroll

    def body(j, carry):
        group(j, False)
        return carry

    lax.fori_loop(0, n_groups - 1, body, 0)
    group(n_groups - 1, True)
    acc = acc_scr[...]
    out = acc / acc[:, A_HD:A_HD + 1]
    lane_q = lax.broadcasted_iota(jnp.int32, (tq, LANES), 1)
    for pair in range(A_GROUP // 2):
        even = out[(2 * pair) * tq:(2 * pair + 1) * tq]
        odd = out[(2 * pair + 1) * tq:(2 * pair + 2) * tq]
        both = jnp.where(lane_q < A_HD, even, pltpu.roll(odd, A_HD, 1))
        o_ref[0, :, pair * LANES:(pair + 1) * LANES] = both.astype(o_ref.dtype)


def _attention(q, k2, v2, tq, tkc, unroll):
    b, _, l, _ = q.shape
    nkc = l // tkc
    assert unroll % 2 == 0 and nkc % unroll == 0
    kern = functools.partial(_attn_kernel, tq=tq, tkc=tkc, nkc=nkc, unroll=unroll)
    kv_map = lambda bi, g, i: (bi, 0, g)
    return pl.pallas_call(
        kern,
        grid=(b, A_KV_HEADS, l // tq),
        in_specs=[
            pl.BlockSpec((1, A_GROUP, tq, LANES), lambda bi, g, i: (bi, g, i, 0)),
            pl.BlockSpec((1, l, LANES), kv_map),
            pl.BlockSpec((1, l, LANES), kv_map),
        ],
        out_specs=pl.BlockSpec((1, tq, A_GROUP * A_HD), lambda bi, g, i: (bi, i, g)),
        out_shape=jax.ShapeDtypeStruct((b, l, A_Q), BF16),
        scratch_shapes=[
            pltpu.VMEM((A_GROUP * tq, LANES), F32),
            pltpu.VMEM((A_GROUP * tq, LANES), F32),
            pltpu.VMEM((2, A_GROUP * tq, tkc), F32),
        ],
        compiler_params=_cparams(("parallel", "parallel", "arbitrary")),
        name="attention",
    )(q, k2, v2)


TOKEN_TILE = 512
ATTN_TQ = 256
ATTN_TKC = 1024
ATTN_UNROLL = 4


def _prep_weights(norm_mix, norm_mlp, a_w_in, a_b_gate, a_g_head, a_w_out,
                  b_w_qkv, b_g_q, b_g_k, b_w_o, w_up, w_down):
    w_in = a_w_in[0]
    n_gate = 4 * M_HEADS
    wg = jnp.zeros((D_MODEL, LANES), F32).at[:, :n_gate].set(w_in[:, 2 * M_QK + 2 * M_V:])
    bg = jnp.zeros((1, LANES), F32).at[0, :n_gate].set(a_b_gate[0])
    w_qkv = b_w_qkv[0]
    wq = w_qkv[:, :A_Q]
    wk = w_qkv[:, A_Q:A_Q + A_KV_HEADS * A_HD].reshape(D_MODEL, A_KV_HEADS, A_HD)
    wv = w_qkv[:, A_Q + A_KV_HEADS * A_HD:].reshape(D_MODEL, A_KV_HEADS, A_HD)
    dup = lambda w: jnp.pad(w, ((0, 0), (0, 0), (0, LANES - A_HD))).reshape(D_MODEL, A_KV2)
    head_id = jnp.arange(LANES) // A_HD
    gmat = jnp.where(head_id[:, None] == head_id[None, :], 1.0 / A_HD, 0.0)
    q_gain = b_g_q[0] * (A_HD ** -0.5 * math.log2(math.e))
    return dict(
        norm_mix=norm_mix.reshape(2, 1, D_MODEL),
        norm_mlp=norm_mlp.reshape(2, 1, D_MODEL),
        wqk=w_in[:, :2 * M_QK].astype(BF16),
        wv=w_in[:, 2 * M_QK:2 * M_QK + M_V].astype(BF16),
        wo=w_in[:, 2 * M_QK + M_V:2 * M_QK + 2 * M_V].astype(BF16),
        wg=wg.astype(BF16),
        bg=bg,
        g_head=a_g_head[0].reshape(1, M_V),
        w_out=a_w_out[0].astype(BF16),
        wq=wq.astype(BF16),
        wk2=dup(wk).astype(BF16),
        wv2=dup(wv).astype(BF16),
        gmat=gmat.astype(BF16),
        gq=jnp.tile(q_gain, 2).reshape(1, LANES),
        gk=jnp.tile(b_g_k[0], 2).reshape(1, LANES),
        w_o=b_w_o[0].astype(BF16),
        w_up=w_up.astype(BF16),
        w_down=w_down.astype(BF16),
    )


def _trunk(x, w):
    b, l, _ = x.shape
    t = b * l
    tm = TOKEN_TILE
    x2 = x.reshape(t, D_MODEL)
    q, k, v, o, gates = _mlstm_proj(x2, w["norm_mix"][0], w["wqk"], w["wv"], w["wo"], w["wg"], w["bg"], tm)
    hf, hb = _mlstm_scan(q.reshape(b, l, M_QK), k.reshape(b, l, M_QK), v.reshape(b, l, M_V),
                         gates.reshape(b, l, LANES))
    x2 = _out_mlp(_mlstm_out_mlp_kernel, "mlstm_out_mlp", x2, [hf.reshape(t, M_V), hb.reshape(t, M_V), o],
                  [w["g_head"]], w["w_out"], w["norm_mlp"][0], w["w_up"][0], w["w_down"][0], tm)
    cos, sin_a, sin_b = _rope_tables(l)
    qa, k2, v2 = _qkv_proj(x2.reshape(b, l, D_MODEL), w["norm_mix"][1], w["wq"], w["wk2"], w["wv2"],
                           w["gmat"], w["gq"], w["gk"], cos, sin_a, sin_b, tm)
    att = _attention(qa, k2, v2, ATTN_TQ, ATTN_TKC, ATTN_UNROLL)
    x2 = _out_mlp(_attn_out_mlp_kernel, "attn_out_mlp", x2, [att.reshape(t, A_Q)], [], w["w_o"],
                  w["norm_mlp"][1], w["w_up"][1], w["w_down"][1], tm)
    return x2.reshape(b, l, D_MODEL)


def kernel(x_prompt, x_sample, norm_mix, norm_mlp, a_w_in, a_b_gate, a_g_head, a_w_out,
           b_w_qkv, b_g_q, b_g_k, b_w_o, w_up, w_down):
    w = _prep_weights(norm_mix, norm_mlp, a_w_in, a_b_gate, a_g_head, a_w_out,
                      b_w_qkv, b_g_q, b_g_k, b_w_o, w_up, w_down)
    return (_trunk(x_prompt, w), _trunk(x_sample, w))
```

```python
import functools
import math

import jax
import jax.numpy as jnp
from jax import lax
from jax.experimental import pallas as pl
from jax.experimental.pallas import tpu as pltpu

F32 = jnp.float32
BF16 = jnp.bfloat16

D_MODEL = 1024
EPS = 1e-6
GRID_W = 64
M_HEADS = 4
M_DQK = 128
M_DV = 256
M_CHUNK = 128
M_QK = M_HEADS * M_DQK
M_V = M_HEADS * M_DV
A_HEADS = 16
A_KV_HEADS = 4
A_GROUP = 4
A_HD = 64
ROPE_THETA = 10000.0
ROPE_HALF = 32
D_FF = 4096

LANES = 128
VMEM_LIMIT = 56 * 1024 * 1024

NEG_INF = float("-inf")


def _cparams(sem):
    return pltpu.CompilerParams(dimension_semantics=sem, vmem_limit_bytes=VMEM_LIMIT)


def _rms_scale(x):
    return lax.rsqrt(jnp.mean(x * x, axis=-1, keepdims=True) + EPS)


def _mlstm_proj_kernel(x_ref, g_ref, wq_ref, wkt_ref, wv_ref, wo_ref, wg_ref, bg_ref,
                       q_ref, kt_ref, v_ref, o_ref, gate_ref):
    x = x_ref[...]
    h = (x * _rms_scale(x) * g_ref[...]).astype(BF16)
    q = jnp.dot(h, wq_ref[...], preferred_element_type=F32)
    q_ref[...] = (q * (M_DQK ** -0.5)).astype(BF16)
    kt = lax.dot_general(wkt_ref[...], h, (((1,), (1,)), ((), ())), preferred_element_type=F32)
    kt_ref[...] = kt.astype(BF16)
    v_ref[...] = jnp.dot(h, wv_ref[...], preferred_element_type=F32).astype(BF16)
    o_ref[...] = jnp.dot(h, wo_ref[...], preferred_element_type=F32)
    gate_ref[...] = jnp.dot(h, wg_ref[...], preferred_element_type=F32) + bg_ref[...]


def _mlstm_proj(x2, g, wq, wkt, wv, wo, wg, bg, tm):
    t = x2.shape[0]
    const = lambda i: (0, 0)
    row = lambda i: (i, 0)
    return pl.pallas_call(
        _mlstm_proj_kernel,
        grid=(t // tm,),
        in_specs=[
            pl.BlockSpec((tm, D_MODEL), row),
            pl.BlockSpec((1, D_MODEL), const),
            pl.BlockSpec((D_MODEL, M_QK), const),
            pl.BlockSpec((M_QK, D_MODEL), const),
            pl.BlockSpec((D_MODEL, M_V), const),
            pl.BlockSpec((D_MODEL, M_V), const),
            pl.BlockSpec((D_MODEL, LANES), const),
            pl.BlockSpec((1, LANES), const),
        ],
        out_specs=[
            pl.BlockSpec((tm, M_QK), row),
            pl.BlockSpec((M_QK, tm), lambda i: (0, i)),
            pl.BlockSpec((tm, M_V), row),
            pl.BlockSpec((tm, M_V), row),
            pl.BlockSpec((tm, LANES), row),
        ],
        out_shape=[
            jax.ShapeDtypeStruct((t, M_QK), BF16),
            jax.ShapeDtypeStruct((M_QK, t), BF16),
            jax.ShapeDtypeStruct((t, M_V), BF16),
            jax.ShapeDtypeStruct((t, M_V), F32),
            jax.ShapeDtypeStruct((t, LANES), F32),
        ],
        compiler_params=_cparams(("parallel",)),
        name="mlstm_proj",
    )(x2, g, wq, wkt, wv, wo, wg, bg)


def _split3_dot(a, b_bf16, dims):
    hi = a.astype(BF16)
    r1 = a - hi.astype(F32)
    mid = r1.astype(BF16)
    lo = (r1 - mid.astype(F32)).astype(BF16)
    dn = (dims, ((), ()))
    out = lax.dot_general(hi, b_bf16, dn, preferred_element_type=F32)
    out = out + lax.dot_general(mid, b_bf16, dn, preferred_element_type=F32)
    out = out + lax.dot_general(lo, b_bf16, dn, preferred_element_type=F32)
    return out


N_GATE_ROWS = 4 * M_HEADS


def _log_sigmoid(x):
    return jnp.minimum(x, 0.0) - jnp.log1p(jnp.exp(-jnp.abs(x)))


def _mlstm_scan_kernel(qf_ref, ktf_ref, vf_ref, gf_ref, qb_ref, ktb_ref, vb_ref, gb_ref,
                       hf_ref, hb_ref, c_scr, m_scr):
    c = pl.program_id(1)

    @pl.when(c == 0)
    def _():
        c_scr[...] = jnp.zeros_like(c_scr)
        m_scr[...] = jnp.full_like(m_scr, NEG_INF)

    row_i = lax.broadcasted_iota(jnp.int32, (M_CHUNK, M_CHUNK), 0)
    col_i = lax.broadcasted_iota(jnp.int32, (M_CHUNK, M_CHUNK), 1)
    lower = row_i >= col_i
    upper = row_i <= col_i
    lane_i = lax.broadcasted_iota(jnp.int32, (M_CHUNK, LANES), 1)
    ones_col = jnp.where(lane_i == 0, 1.0, 0.0).astype(BF16)

    dirs = (
        (0, qf_ref, ktf_ref, vf_ref, gf_ref, hf_ref, lower),
        (1, qb_ref, ktb_ref, vb_ref, gb_ref, hb_ref, upper),
    )
    units = [dict(d=d, hd=hd, u=d * M_HEADS + hd, q_ref=t[1], kt_ref=t[2], v_ref=t[3], h_ref=t[5], mask=t[6])
             for d, t in enumerate(dirs) for hd in range(M_HEADS)]

    for d, _, _, _, g_ref, _, mask in dirs:
        tri = jnp.where(mask, 1.0, 0.0).astype(BF16)
        gates = g_ref[...]
        ls = _log_sigmoid(gates)
        gates_t = gates.T[:N_GATE_ROWS]
        ls_t = ls.T[:N_GATE_ROWS]
        cum_row = _split3_dot(ls_t, tri, ((1,), (1,)))
        cum_c = jnp.concatenate([cum_row, jnp.zeros((M_CHUNK - N_GATE_ROWS, M_CHUNK), F32)], axis=0).T
        for un in units[d * M_HEADS:(d + 1) * M_HEADS]:
            li_idx = 8 * d + un["hd"]
            lf_idx = li_idx + M_HEADS
            un["a_row"] = gates_t[li_idx:li_idx + 1, :] - cum_row[lf_idx:lf_idx + 1, :]
            un["b_col"] = cum_c[:, lf_idx:lf_idx + 1]
            un["b_last"] = jnp.sum(ls_t[lf_idx:lf_idx + 1, :], axis=-1, keepdims=True)

        for un in units[d * M_HEADS:(d + 1) * M_HEADS]:
            hd = un["hd"]
            q = un["q_ref"][0, :, hd * M_DQK:(hd + 1) * M_DQK]
            kt = un["kt_ref"][hd * M_DQK:(hd + 1) * M_DQK, :]
            v = un["v_ref"][0, :, hd * M_DV:(hd + 1) * M_DV]
            v_ext = jnp.concatenate([v, ones_col], axis=1)
            c_old = c_scr[un["u"]]
            m_old = m_scr[un["u"]]
            un["v_ext"], un["m_old"] = v_ext, m_old
            un["s_qk"] = jnp.dot(q, kt, preferred_element_type=F32)
            un["q_c"] = jnp.dot(q, c_old.astype(BF16), preferred_element_type=F32)

            a_row, b_last = un["a_row"], un["b_last"]
            m_new = b_last + jnp.maximum(m_old, jnp.max(a_row, axis=-1, keepdims=True))
            decay = jnp.exp(b_last + m_old - m_new)
            wk_row = jnp.exp(b_last + a_row - m_new)
            kw_t = (kt.astype(F32) * wk_row).astype(BF16)
            c_scr[un["u"]] = decay * c_old + jnp.dot(kw_t, v_ext, preferred_element_type=F32)
            m_scr[un["u"]] = m_new

    for un in units:
        a_masked = jnp.where(un["mask"], un["a_row"], NEG_INF)
        e_col = jnp.maximum(un["m_old"], jnp.max(a_masked, axis=-1, keepdims=True))
        un["m_row"] = un["b_col"] + e_col
        un["w_inter"] = jnp.exp(un["m_old"] - e_col)
        un["p"] = (jnp.exp(a_masked - e_col) * un["s_qk"]).astype(BF16)

    for un in units:
        hd = un["hd"]
        num = un["w_inter"] * un["q_c"] + jnp.dot(un["p"], un["v_ext"], preferred_element_type=F32)
        den = num[:, M_DV:M_DV + 1]
        h = num[:, :M_DV] / jnp.maximum(jnp.abs(den), jnp.exp(-un["m_row"]))
        un["h_ref"][0, :, hd * M_DV:(hd + 1) * M_DV] = h


def _mlstm_scan(q, kt, v, gates):
    b, l, _ = q.shape
    nc = l // M_CHUNK
    fwd = lambda bi, ci: (bi, ci, 0)
    bwd = lambda bi, ci: (bi, nc - 1 - ci, 0)
    specs = []
    for im in (fwd, bwd):
        kt_map = lambda bi, ci, im=im: (0, bi * nc + im(bi, ci)[1])
        specs += [
            pl.BlockSpec((1, M_CHUNK, M_QK), im),
            pl.BlockSpec((M_QK, M_CHUNK), kt_map),
            pl.BlockSpec((1, M_CHUNK, M_V), im),
            pl.BlockSpec((None, M_CHUNK, LANES), im),
        ]
    return pl.pallas_call(
        _mlstm_scan_kernel,
        grid=(b, nc),
        in_specs=specs,
        out_specs=[pl.BlockSpec((1, M_CHUNK, M_V), fwd), pl.BlockSpec((1, M_CHUNK, M_V), bwd)],
        out_shape=[jax.ShapeDtypeStruct((b, l, M_V), F32), jax.ShapeDtypeStruct((b, l, M_V), F32)],
        scratch_shapes=[
            pltpu.VMEM((2 * M_HEADS, M_DQK, M_DV + LANES), F32),
            pltpu.VMEM((2 * M_HEADS, 1, 1), F32),
        ],
        compiler_params=_cparams(("parallel", "arbitrary")),
        name="mlstm_scan",
    )(q, kt, v, gates, q, kt, v, gates)


FF_CHUNK = 1024


def _mlp_residual(x, g_ref, wup_ref, wdn_ref):
    h = (x * _rms_scale(x) * g_ref[...]).astype(BF16)
    acc = x
    for f in range(D_FF // FF_CHUNK):
        u = jnp.dot(h, wup_ref[:, f * FF_CHUNK:(f + 1) * FF_CHUNK], preferred_element_type=F32)
        u = jnp.maximum(u, 0.0)
        u = (u * u).astype(BF16)
        acc = acc + jnp.dot(u, wdn_ref[f * FF_CHUNK:(f + 1) * FF_CHUNK, :], preferred_element_type=F32)
    return acc


def _mlstm_out_mlp_kernel(x_ref, hf_ref, hb_ref, o_ref, gh_ref, w_ref, g_ref, wup_ref, wdn_ref, y_ref):
    hs = hf_ref[...] + hb_ref[...]
    parts = []
    for hd in range(M_HEADS):
        blk = hs[:, hd * M_DV:(hd + 1) * M_DV]
        parts.append(blk * _rms_scale(blk))
    hn = jnp.concatenate(parts, axis=1) * gh_ref[...]
    y = (jax.nn.sigmoid(o_ref[...]) * hn).astype(BF16)
    x1 = x_ref[...] + jnp.dot(y, w_ref[...], preferred_element_type=F32)
    y_ref[...] = _mlp_residual(x1, g_ref, wup_ref, wdn_ref)


def _attn_out_mlp_kernel(x_ref, a_ref, w_ref, g_ref, wup_ref, wdn_ref, y_ref):
    x1 = x_ref[...] + jnp.dot(a_ref[...], w_ref[...], preferred_element_type=F32)
    y_ref[...] = _mlp_residual(x1, g_ref, wup_ref, wdn_ref)


def _out_mlp(body, name, x2, acts, consts, w_out, g, wup, wdn, tm):
    t = x2.shape[0]
    const = lambda i: (0, 0)
    row = lambda i: (i, 0)
    once = dict(pipeline_mode=pl.Buffered(1))
    return pl.pallas_call(
        body,
        grid=(t // tm,),
        in_specs=(
            [pl.BlockSpec((tm, D_MODEL), row)]
            + [pl.BlockSpec((tm, a.shape[1]), row) for a in acts]
            + [pl.BlockSpec(c.shape, const, **once) for c in consts]
            + [
                pl.BlockSpec(w_out.shape, const, **once),
                pl.BlockSpec((1, D_MODEL), const, **once),
                pl.BlockSpec((D_MODEL, D_FF), const, **once),
                pl.BlockSpec((D_FF, D_MODEL), const, **once),
            ]
        ),
        out_specs=pl.BlockSpec((tm, D_MODEL), row),
        out_shape=jax.ShapeDtypeStruct((t, D_MODEL), F32),
        compiler_params=_cparams(("parallel",)),
        name=name,
    )(x2, *acts, *consts, w_out, g, wup, wdn)


A_Q = A_HEADS * A_HD
A_KV2 = A_KV_HEADS * LANES


def _headnorm_rope(x, gmat, gain, cos, sin_a, sin_b):
    ms = jnp.dot((x * x).astype(BF16), gmat, preferred_element_type=F32)
    y = x * lax.rsqrt(ms + EPS) * gain
    return y * cos + pltpu.roll(y, LANES - 16, 1) * sin_a + pltpu.roll(y, 16, 1) * sin_b


def _qkv_proj_kernel(x_ref, g_ref, wq_ref, wk_ref, wv_ref, gmat_ref, gq_ref, gk_ref,
                     cos_ref, sina_ref, sinb_ref, q_ref, k_ref, v_ref):
    x = x_ref[0]
    h = (x * _rms_scale(x) * g_ref[...]).astype(BF16)
    gmat = gmat_ref[...]
    cos, sin_a, sin_b = cos_ref[...], sina_ref[...], sinb_ref[...]
    q = jnp.dot(h, wq_ref[...], preferred_element_type=F32)
    for j in range(A_Q // LANES):
        blk = _headnorm_rope(q[:, j * LANES:(j + 1) * LANES], gmat, gq_ref[...], cos, sin_a, sin_b)
        q_ref[0, 2 * j] = blk.astype(BF16)
        q_ref[0, 2 * j + 1] = pltpu.roll(blk, A_HD, 1).astype(BF16)
    k = jnp.dot(h, wk_ref[...], preferred_element_type=F32)
    for j in range(A_KV2 // LANES):
        blk = _headnorm_rope(k[:, j * LANES:(j + 1) * LANES], gmat, gk_ref[...], cos, sin_a, sin_b)
        k_ref[0, :, j * LANES:(j + 1) * LANES] = blk.astype(BF16)
    v = jnp.dot(h, wv_ref[...], preferred_element_type=F32)
    lane = lax.broadcasted_iota(jnp.int32, v.shape, 1)
    v_ref[0] = jnp.where(lane % LANES == A_HD, 1.0, v).astype(BF16)


def _qkv_proj(x, g, wq, wk2, wv2, gmat, gq, gk, cos, sin_a, sin_b, tm):
    b, l, _ = x.shape
    const = lambda bi, i: (0, 0)
    tok = lambda bi, i: (bi, i, 0)
    pos = lambda bi, i: (i, 0)
    return pl.pallas_call(
        _qkv_proj_kernel,
        grid=(b, l // tm),
        in_specs=[
            pl.BlockSpec((1, tm, D_MODEL), tok),
            pl.BlockSpec((1, D_MODEL), const),
            pl.BlockSpec((D_MODEL, A_Q), const),
            pl.BlockSpec((D_MODEL, A_KV2), const),
            pl.BlockSpec((D_MODEL, A_KV2), const),
            pl.BlockSpec((LANES, LANES), const),
            pl.BlockSpec((1, LANES), const),
            pl.BlockSpec((1, LANES), const),
            pl.BlockSpec((tm, LANES), pos),
            pl.BlockSpec((tm, LANES), pos),
            pl.BlockSpec((tm, LANES), pos),
        ],
        out_specs=[
            pl.BlockSpec((1, A_HEADS, tm, LANES), lambda bi, i: (bi, 0, i, 0)),
            pl.BlockSpec((1, tm, A_KV2), tok),
            pl.BlockSpec((1, tm, A_KV2), tok),
        ],
        out_shape=[
            jax.ShapeDtypeStruct((b, A_HEADS, l, LANES), BF16),
            jax.ShapeDtypeStruct((b, l, A_KV2), BF16),
            jax.ShapeDtypeStruct((b, l, A_KV2), BF16),
        ],
        compiler_params=_cparams(("parallel", "parallel")),
        name="qkv_proj",
    )(x, g, wq, wk2, wv2, gmat, gq, gk, cos, sin_a, sin_b)


def _rope_tables(l):
    pos = jnp.arange(l)
    freqs = ROPE_THETA ** (-jnp.arange(0, ROPE_HALF, 2, dtype=F32) / ROPE_HALF)
    ang_r = (pos // GRID_W).astype(F32)[:, None] * freqs[None, :]
    ang_c = (pos % GRID_W).astype(F32)[:, None] * freqs[None, :]
    emb = jnp.concatenate([ang_r, ang_r, ang_c, ang_c], axis=-1)
    cos = jnp.cos(emb)
    sin = jnp.sin(emb)
    first = (jnp.arange(A_HD) % ROPE_HALF) < (ROPE_HALF // 2)
    sin_a = jnp.where(first, -sin, 0.0)
    sin_b = jnp.where(first, 0.0, sin)
    tile2 = lambda a: jnp.concatenate([a, a], axis=-1)
    return tile2(cos), tile2(sin_a), tile2(sin_b)


def _attn_kernel(q_ref, k_ref, v_ref, o_ref, m_scr, acc_scr, s_scr, p_scr, *, tq, tkc, nkc, unroll):
    q = q_ref[0].reshape(A_GROUP * tq, LANES)
    m_scr[...] = jnp.full_like(m_scr, NEG_INF)
    acc_scr[...] = jnp.zeros_like(acc_scr)

    def scores(c):
        kk = k_ref[0, pl.ds(pl.multiple_of(c * tkc, tkc), tkc), :]
        return lax.dot_general(q, kk, (((1,), (1,)), ((), ())), preferred_element_type=F32)

    def softmax_pv(s_ref, p_ref, c):
        vv = v_ref[0, pl.ds(pl.multiple_of(c * tkc, tkc), tkc), :]
        blocks = [s_ref[:, j * LANES:(j + 1) * LANES] for j in range(tkc // LANES)]
        blk_max = functools.reduce(jnp.maximum, blocks)
        m_prev = m_scr[...]
        m_new = jnp.maximum(m_prev, jnp.max(blk_max, axis=-1, keepdims=True))
        alpha = jnp.exp2(m_prev - m_new)
        for j, blk in enumerate(blocks):
            p_ref[:, j * LANES:(j + 1) * LANES] = jnp.exp2(blk - m_new).astype(BF16)
        acc_scr[...] = alpha * acc_scr[...] + jnp.dot(p_ref[...], vv, preferred_element_type=F32)
        m_scr[...] = m_new

    def group(j, last):
        for u in range(unroll):
            c = j * unroll + u
            if not (last and u == unroll - 1):
                s_scr[(u + 1) % 2] = scores(c + 1)
            softmax_pv(s_scr.at[u % 2], p_scr.at[u % 2], c)

    s_scr[0] = scores(0)
    n_groups = nkc // unroll

    def body(j, carry):
        group(j, False)
        return carry

    lax.fori_loop(0, n_groups - 1, body, 0)
    group(n_groups - 1, True)
    acc = acc_scr[...]
    out = acc / acc[:, A_HD:A_HD + 1]
    lane_q = lax.broadcasted_iota(jnp.int32, (tq, LANES), 1)
    for pair in range(A_GROUP // 2):
        even = out[(2 * pair) * tq:(2 * pair + 1) * tq]
        odd = out[(2 * pair + 1) * tq:(2 * pair + 2) * tq]
        both = jnp.where(lane_q < A_HD, even, pltpu.roll(odd, A_HD, 1))
        o_ref[0, :, pair * LANES:(pair + 1) * LANES] = both.astype(o_ref.dtype)


def _attention(q, k2, v2, tq, tkc, unroll):
    b, _, l, _ = q.shape
    nkc = l // tkc
    assert unroll % 2 == 0 and nkc % unroll == 0
    kern = functools.partial(_attn_kernel, tq=tq, tkc=tkc, nkc=nkc, unroll=unroll)
    kv_map = lambda bi, g, i: (bi, 0, g)
    return pl.pallas_call(
        kern,
        grid=(b, A_KV_HEADS, l // tq),
        in_specs=[
            pl.BlockSpec((1, A_GROUP, tq, LANES), lambda bi, g, i: (bi, g, i, 0)),
            pl.BlockSpec((1, l, LANES), kv_map),
            pl.BlockSpec((1, l, LANES), kv_map),
        ],
        out_specs=pl.BlockSpec((1, tq, A_GROUP * A_HD), lambda bi, g, i: (bi, i, g)),
        out_shape=jax.ShapeDtypeStruct((b, l, A_Q), BF16),
        scratch_shapes=[
            pltpu.VMEM((A_GROUP * tq, LANES), F32),
            pltpu.VMEM((A_GROUP * tq, LANES), F32),
            pltpu.VMEM((2, A_GROUP * tq, tkc), F32),
            pltpu.VMEM((2, A_GROUP * tq, tkc), BF16),
        ],
        compiler_params=_cparams(("parallel", "parallel", "arbitrary")),
        name="attention",
    )(q, k2, v2)


TOKEN_TILE = 512
ATTN_TQ = 256
ATTN_TKC = 1024
ATTN_MAX_UNROLL = 8


def _prep_weights(norm_mix, norm_mlp, a_w_in, a_b_gate, a_g_head, a_w_out,
                  b_w_qkv, b_g_q, b_g_k, b_w_o, w_up, w_down):
    w_in = a_w_in[0]
    n_gate = 4 * M_HEADS
    wg = jnp.zeros((D_MODEL, LANES), F32).at[:, :n_gate].set(w_in[:, 2 * M_QK + 2 * M_V:])
    bg = jnp.zeros((1, LANES), F32).at[0, :n_gate].set(a_b_gate[0])
    w_qkv = b_w_qkv[0]
    wq = w_qkv[:, :A_Q]
    wk = w_qkv[:, A_Q:A_Q + A_KV_HEADS * A_HD].reshape(D_MODEL, A_KV_HEADS, A_HD)
    wv = w_qkv[:, A_Q + A_KV_HEADS * A_HD:].reshape(D_MODEL, A_KV_HEADS, A_HD)
    dup = lambda w: jnp.pad(w, ((0, 0), (0, 0), (0, LANES - A_HD))).reshape(D_MODEL, A_KV2)
    head_id = jnp.arange(LANES) // A_HD
    gmat = jnp.where(head_id[:, None] == head_id[None, :], 1.0 / A_HD, 0.0)
    q_gain = b_g_q[0] * (A_HD ** -0.5 * math.log2(math.e))
    return dict(
        norm_mix=norm_mix.reshape(2, 1, D_MODEL),
        norm_mlp=norm_mlp.reshape(2, 1, D_MODEL),
        wq=w_in[:, :M_QK].astype(BF16),
        wkt=w_in[:, M_QK:2 * M_QK].T.astype(BF16),
        wv=w_in[:, 2 * M_QK:2 * M_QK + M_V].astype(BF16),
        wo=w_in[:, 2 * M_QK + M_V:2 * M_QK + 2 * M_V].astype(BF16),
        wg=wg.astype(BF16),
        bg=bg,
        g_head=a_g_head[0].reshape(1, M_V),
        w_out=a_w_out[0].astype(BF16),
        wq_attn=wq.astype(BF16),
        wk2=dup(wk).astype(BF16),
        wv2=dup(wv).astype(BF16),
        gmat=gmat.astype(BF16),
        gq=jnp.tile(q_gain, 2).reshape(1, LANES),
        gk=jnp.tile(b_g_k[0], 2).reshape(1, LANES),
        w_o=b_w_o[0].astype(BF16),
        w_up=w_up.astype(BF16),
        w_down=w_down.astype(BF16),
    )


def _trunk(x, w, rope):
    b, l, _ = x.shape
    t = b * l
    tm = TOKEN_TILE
    x2 = x.reshape(t, D_MODEL)
    q, kt, v, o, gates = _mlstm_proj(x2, w["norm_mix"][0], w["wq"], w["wkt"], w["wv"], w["wo"], w["wg"],
                                     w["bg"], tm)
    hf, hb = _mlstm_scan(q.reshape(b, l, M_QK), kt, v.reshape(b, l, M_V), gates.reshape(b, l, LANES))
    x2 = _out_mlp(_mlstm_out_mlp_kernel, "mlstm_out_mlp", x2, [hf.reshape(t, M_V), hb.reshape(t, M_V), o],
                  [w["g_head"]], w["w_out"], w["norm_mlp"][0], w["w_up"][0], w["w_down"][0], tm)
    cos, sin_a, sin_b = rope
    qa, k2, v2 = _qkv_proj(x2.reshape(b, l, D_MODEL), w["norm_mix"][1], w["wq_attn"], w["wk2"], w["wv2"],
                           w["gmat"], w["gq"], w["gk"], cos, sin_a, sin_b, tm)
    att = _attention(qa, k2, v2, ATTN_TQ, ATTN_TKC, min(ATTN_MAX_UNROLL, l // ATTN_TKC // 2))
    x2 = _out_mlp(_attn_out_mlp_kernel, "attn_out_mlp", x2, [att.reshape(t, A_Q)], [], w["w_o"],
                  w["norm_mlp"][1], w["w_up"][1], w["w_down"][1], tm)
    return x2.reshape(b, l, D_MODEL)


def kernel(x_prompt, x_sample, norm_mix, norm_mlp, a_w_in, a_b_gate, a_g_head, a_w_out,
           b_w_qkv, b_g_q, b_g_k, b_w_o, w_up, w_down):
    w = _prep_weights(norm_mix, norm_mlp, a_w_in, a_b_gate, a_g_head, a_w_out,
                      b_w_qkv, b_g_q, b_g_k, b_w_o, w_up, w_down)
    rope = _rope_tables(max(x_prompt.shape[1], x_sample.shape[1]))
    return (_trunk(x_prompt, w, rope), _trunk(x_sample, w, rope))
```

```python
import functools
import math

import jax
import jax.numpy as jnp
from jax import lax
from jax.experimental import pallas as pl
from jax.experimental.pallas import tpu as pltpu

F32 = jnp.float32
BF16 = jnp.bfloat16

D_MODEL = 1024
EPS = 1e-6
GRID_W = 64
M_HEADS = 4
M_DQK = 128
M_DV = 256
M_CHUNK = 128
M_QK = M_HEADS * M_DQK
M_V = M_HEADS * M_DV
A_HEADS = 16
A_KV_HEADS = 4
A_GROUP = 4
A_HD = 64
ROPE_THETA = 10000.0
ROPE_HALF = 32
D_FF = 4096

LANES = 128
VMEM_LIMIT = 56 * 1024 * 1024

NEG_INF = float("-inf")


def _cparams(sem):
    return pltpu.CompilerParams(dimension_semantics=sem, vmem_limit_bytes=VMEM_LIMIT)


def _rms_scale(x):
    return lax.rsqrt(jnp.mean(x * x, axis=-1, keepdims=True) + EPS)


def _mlstm_proj_kernel(x_ref, g_ref, wq_ref, wkt_ref, wv_ref, wo_ref, wg_ref, bg_ref,
                       q_ref, kt_ref, v_ref, o_ref, gate_ref):
    x = x_ref[...]
    h = (x * _rms_scale(x) * g_ref[...]).astype(BF16)
    q = jnp.dot(h, wq_ref[...], preferred_element_type=F32)
    q_ref[...] = (q * (M_DQK ** -0.5)).astype(BF16)
    kt = lax.dot_general(wkt_ref[...], h, (((1,), (1,)), ((), ())), preferred_element_type=F32)
    kt_ref[...] = kt.astype(BF16)
    v_ref[...] = jnp.dot(h, wv_ref[...], preferred_element_type=F32).astype(BF16)
    o_ref[...] = jnp.dot(h, wo_ref[...], preferred_element_type=F32)
    gate_ref[...] = jnp.dot(h, wg_ref[...], preferred_element_type=F32) + bg_ref[...]


def _mlstm_proj(x2, g, wq, wkt, wv, wo, wg, bg, tm):
    t = x2.shape[0]
    const = lambda i: (0, 0)
    row = lambda i: (i, 0)
    return pl.pallas_call(
        _mlstm_proj_kernel,
        grid=(t // tm,),
        in_specs=[
            pl.BlockSpec((tm, D_MODEL), row),
            pl.BlockSpec((1, D_MODEL), const),
            pl.BlockSpec((D_MODEL, M_QK), const),
            pl.BlockSpec((M_QK, D_MODEL), const),
            pl.BlockSpec((D_MODEL, M_V), const),
            pl.BlockSpec((D_MODEL, M_V), const),
            pl.BlockSpec((D_MODEL, LANES), const),
            pl.BlockSpec((1, LANES), const),
        ],
        out_specs=[
            pl.BlockSpec((tm, M_QK), row),
            pl.BlockSpec((M_QK, tm), lambda i: (0, i)),
            pl.BlockSpec((tm, M_V), row),
            pl.BlockSpec((tm, M_V), row),
            pl.BlockSpec((tm, LANES), row),
        ],
        out_shape=[
            jax.ShapeDtypeStruct((t, M_QK), BF16),
            jax.ShapeDtypeStruct((M_QK, t), BF16),
            jax.ShapeDtypeStruct((t, M_V), BF16),
            jax.ShapeDtypeStruct((t, M_V), F32),
            jax.ShapeDtypeStruct((t, LANES), F32),
        ],
        compiler_params=_cparams(("parallel",)),
        name="mlstm_proj",
    )(x2, g, wq, wkt, wv, wo, wg, bg)


def _split3_dot(a, b_bf16, dims):
    hi = a.astype(BF16)
    r1 = a - hi.astype(F32)
    mid = r1.astype(BF16)
    lo = (r1 - mid.astype(F32)).astype(BF16)
    dn = (dims, ((), ()))
    out = lax.dot_general(hi, b_bf16, dn, preferred_element_type=F32)
    out = out + lax.dot_general(mid, b_bf16, dn, preferred_element_type=F32)
    out = out + lax.dot_general(lo, b_bf16, dn, preferred_element_type=F32)
    return out


N_GATE_ROWS = 4 * M_HEADS


def _log_sigmoid(x):
    return jnp.minimum(x, 0.0) - jnp.log1p(jnp.exp(-jnp.abs(x)))


def _mlstm_scan_kernel(qf_ref, ktf_ref, vf_ref, gf_ref, qb_ref, ktb_ref, vb_ref, gb_ref,
                       hf_ref, hb_ref, c_scr, m_scr):
    c = pl.program_id(1)

    @pl.when(c == 0)
    def _():
        c_scr[...] = jnp.zeros_like(c_scr)
        m_scr[...] = jnp.full_like(m_scr, NEG_INF)

    row_i = lax.broadcasted_iota(jnp.int32, (M_CHUNK, M_CHUNK), 0)
    col_i = lax.broadcasted_iota(jnp.int32, (M_CHUNK, M_CHUNK), 1)
    lower = row_i >= col_i
    upper = row_i <= col_i
    lane_i = lax.broadcasted_iota(jnp.int32, (M_CHUNK, LANES), 1)
    ones_col = jnp.where(lane_i == 0, 1.0, 0.0).astype(BF16)

    dirs = (
        (0, qf_ref, ktf_ref, vf_ref, gf_ref, hf_ref, lower),
        (1, qb_ref, ktb_ref, vb_ref, gb_ref, hb_ref, upper),
    )
    units = [dict(d=d, hd=hd, u=d * M_HEADS + hd, q_ref=t[1], kt_ref=t[2], v_ref=t[3], h_ref=t[5], mask=t[6])
             for d, t in enumerate(dirs) for hd in range(M_HEADS)]

    for d, _, _, _, g_ref, _, mask in dirs:
        tri = jnp.where(mask, 1.0, 0.0).astype(BF16)
        gates = g_ref[...]
        ls = _log_sigmoid(gates)
        gates_t = gates.T[:N_GATE_ROWS]
        ls_t = ls.T[:N_GATE_ROWS]
        cum_row = _split3_dot(ls_t, tri, ((1,), (1,)))
        cum_c = jnp.concatenate([cum_row, jnp.zeros((M_CHUNK - N_GATE_ROWS, M_CHUNK), F32)], axis=0).T
        for un in units[d * M_HEADS:(d + 1) * M_HEADS]:
            li_idx = 8 * d + un["hd"]
            lf_idx = li_idx + M_HEADS
            un["a_row"] = gates_t[li_idx:li_idx + 1, :] - cum_row[lf_idx:lf_idx + 1, :]
            un["b_col"] = cum_c[:, lf_idx:lf_idx + 1]
            un["b_last"] = jnp.sum(ls_t[lf_idx:lf_idx + 1, :], axis=-1, keepdims=True)

        for un in units[d * M_HEADS:(d + 1) * M_HEADS]:
            hd = un["hd"]
            q = un["q_ref"][0, :, hd * M_DQK:(hd + 1) * M_DQK]
            kt = un["kt_ref"][hd * M_DQK:(hd + 1) * M_DQK, :]
            v = un["v_ref"][0, :, hd * M_DV:(hd + 1) * M_DV]
            v_ext = jnp.concatenate([v, ones_col], axis=1)
            c_old = c_scr[un["u"]]
            m_old = m_scr[un["u"]]
            un["v_ext"], un["m_old"] = v_ext, m_old
            un["s_qk"] = jnp.dot(q, kt, preferred_element_type=F32)
            un["q_c"] = jnp.dot(q, c_old.astype(BF16), preferred_element_type=F32)

            a_row, b_last = un["a_row"], un["b_last"]
            m_new = b_last + jnp.maximum(m_old, jnp.max(a_row, axis=-1, keepdims=True))
            decay = jnp.exp(b_last + m_old - m_new)
            wk_row = jnp.exp(b_last + a_row - m_new)
            kw_t = (kt.astype(F32) * wk_row).astype(BF16)
            c_scr[un["u"]] = decay * c_old + jnp.dot(kw_t, v_ext, preferred_element_type=F32)
            m_scr[un["u"]] = m_new

    for un in units:
        a_masked = jnp.where(un["mask"], un["a_row"], NEG_INF)
        e_col = jnp.maximum(un["m_old"], jnp.max(a_masked, axis=-1, keepdims=True))
        un["m_row"] = un["b_col"] + e_col
        un["w_inter"] = jnp.exp(un["m_old"] - e_col)
        un["p"] = (jnp.exp(a_masked - e_col) * un["s_qk"]).astype(BF16)

    for un in units:
        hd = un["hd"]
        num = un["w_inter"] * un["q_c"] + jnp.dot(un["p"], un["v_ext"], preferred_element_type=F32)
        den = num[:, M_DV:M_DV + 1]
        h = num[:, :M_DV] / jnp.maximum(jnp.abs(den), jnp.exp(-un["m_row"]))
        un["h_ref"][0, :, hd * M_DV:(hd + 1) * M_DV] = h


def _mlstm_scan(q, kt, v, gates):
    b, l, _ = q.shape
    nc = l // M_CHUNK
    fwd = lambda bi, ci: (bi, ci, 0)
    bwd = lambda bi, ci: (bi, nc - 1 - ci, 0)
    specs = []
    for im in (fwd, bwd):
        kt_map = lambda bi, ci, im=im: (0, bi * nc + im(bi, ci)[1])
        specs += [
            pl.BlockSpec((1, M_CHUNK, M_QK), im),
            pl.BlockSpec((M_QK, M_CHUNK), kt_map),
            pl.BlockSpec((1, M_CHUNK, M_V), im),
            pl.BlockSpec((None, M_CHUNK, LANES), im),
        ]
    return pl.pallas_call(
        _mlstm_scan_kernel,
        grid=(b, nc),
        in_specs=specs,
        out_specs=[pl.BlockSpec((1, M_CHUNK, M_V), fwd), pl.BlockSpec((1, M_CHUNK, M_V), bwd)],
        out_shape=[jax.ShapeDtypeStruct((b, l, M_V), F32), jax.ShapeDtypeStruct((b, l, M_V), F32)],
        scratch_shapes=[
            pltpu.VMEM((2 * M_HEADS, M_DQK, M_DV + LANES), F32),
            pltpu.VMEM((2 * M_HEADS, 1, 1), F32),
        ],
        compiler_params=_cparams(("parallel", "arbitrary")),
        name="mlstm_scan",
    )(q, kt, v, gates, q, kt, v, gates)


FF_CHUNK = 1024


def _mlp_residual(x, g_ref, wup_ref, wdn_ref):
    h = (x * _rms_scale(x) * g_ref[...]).astype(BF16)
    acc = x
    for f in range(D_FF // FF_CHUNK):
        u = jnp.dot(h, wup_ref[:, f * FF_CHUNK:(f + 1) * FF_CHUNK], preferred_element_type=F32)
        u = jnp.maximum(u, 0.0)
        u = (u * u).astype(BF16)
        acc = acc + jnp.dot(u, wdn_ref[f * FF_CHUNK:(f + 1) * FF_CHUNK, :], preferred_element_type=F32)
    return acc


def _mlstm_out_mlp_kernel(x_ref, hf_ref, hb_ref, o_ref, gh_ref, w_ref, g_ref, wup_ref, wdn_ref, y_ref):
    hs = hf_ref[...] + hb_ref[...]
    parts = []
    for hd in range(M_HEADS):
        blk = hs[:, hd * M_DV:(hd + 1) * M_DV]
        parts.append(blk * _rms_scale(blk))
    hn = jnp.concatenate(parts, axis=1) * gh_ref[...]
    y = (jax.nn.sigmoid(o_ref[...]) * hn).astype(BF16)
    x1 = x_ref[...] + jnp.dot(y, w_ref[...], preferred_element_type=F32)
    y_ref[...] = _mlp_residual(x1, g_ref, wup_ref, wdn_ref)


def _attn_out_mlp_kernel(x_ref, a_ref, w_ref, g_ref, wup_ref, wdn_ref, y_ref):
    x1 = x_ref[...] + jnp.dot(a_ref[...], w_ref[...], preferred_element_type=F32)
    y_ref[...] = _mlp_residual(x1, g_ref, wup_ref, wdn_ref)


def _out_mlp(body, name, x2, acts, consts, w_out, g, wup, wdn, tm):
    t = x2.shape[0]
    const = lambda i: (0, 0)
    row = lambda i: (i, 0)
    once = dict(pipeline_mode=pl.Buffered(1))
    return pl.pallas_call(
        body,
        grid=(t // tm,),
        in_specs=(
            [pl.BlockSpec((tm, D_MODEL), row)]
            + [pl.BlockSpec((tm, a.shape[1]), row) for a in acts]
            + [pl.BlockSpec(c.shape, const, **once) for c in consts]
            + [
                pl.BlockSpec(w_out.shape, const, **once),
                pl.BlockSpec((1, D_MODEL), const, **once),
                pl.BlockSpec((D_MODEL, D_FF), const, **once),
                pl.BlockSpec((D_FF, D_MODEL), const, **once),
            ]
        ),
        out_specs=pl.BlockSpec((tm, D_MODEL), row),
        out_shape=jax.ShapeDtypeStruct((t, D_MODEL), F32),
        compiler_params=_cparams(("parallel",)),
        name=name,
    )(x2, *acts, *consts, w_out, g, wup, wdn)


A_Q = A_HEADS * A_HD
A_KV2 = A_KV_HEADS * LANES


def _headnorm_rope(x, gmat, gain, cos, sin_a, sin_b):
    ms = jnp.dot((x * x).astype(BF16), gmat, preferred_element_type=F32)
    y = x * lax.rsqrt(ms + EPS) * gain
    return y * cos + pltpu.roll(y, LANES - 16, 1) * sin_a + pltpu.roll(y, 16, 1) * sin_b


def _qkv_proj_kernel(x_ref, g_ref, wq_ref, wk_ref, wv_ref, gmat_ref, gq_ref, gk_ref,
                     cos_ref, sina_ref, sinb_ref, q_ref, k_ref, v_ref):
    x = x_ref[0]
    h = (x * _rms_scale(x) * g_ref[...]).astype(BF16)
    gmat = gmat_ref[...]
    cos, sin_a, sin_b = cos_ref[...], sina_ref[...], sinb_ref[...]
    q = jnp.dot(h, wq_ref[...], preferred_element_type=F32)
    for j in range(A_Q // LANES):
        blk = _headnorm_rope(q[:, j * LANES:(j + 1) * LANES], gmat, gq_ref[...], cos, sin_a, sin_b)
        q_ref[0, 2 * j] = blk.astype(BF16)
        q_ref[0, 2 * j + 1] = pltpu.roll(blk, A_HD, 1).astype(BF16)
    k = jnp.dot(h, wk_ref[...], preferred_element_type=F32)
    for j in range(A_KV2 // LANES):
        blk = _headnorm_rope(k[:, j * LANES:(j + 1) * LANES], gmat, gk_ref[...], cos, sin_a, sin_b)
        k_ref[0, :, j * LANES:(j + 1) * LANES] = blk.astype(BF16)
    v = jnp.dot(h, wv_ref[...], preferred_element_type=F32)
    lane = lax.broadcasted_iota(jnp.int32, v.shape, 1)
    v_ref[0] = jnp.where(lane % LANES == A_HD, 1.0, v).astype(BF16)


def _qkv_proj(x, g, wq, wk2, wv2, gmat, gq, gk, cos, sin_a, sin_b, tm):
    b, l, _ = x.shape
    const = lambda bi, i: (0, 0)
    tok = lambda bi, i: (bi, i, 0)
    pos = lambda bi, i: (i, 0)
    return pl.pallas_call(
        _qkv_proj_kernel,
        grid=(b, l // tm),
        in_specs=[
            pl.BlockSpec((1, tm, D_MODEL), tok),
            pl.BlockSpec((1, D_MODEL), const),
            pl.BlockSpec((D_MODEL, A_Q), const),
            pl.BlockSpec((D_MODEL, A_KV2), const),
            pl.BlockSpec((D_MODEL, A_KV2), const),
            pl.BlockSpec((LANES, LANES), const),
            pl.BlockSpec((1, LANES), const),
            pl.BlockSpec((1, LANES), const),
            pl.BlockSpec((tm, LANES), pos),
            pl.BlockSpec((tm, LANES), pos),
            pl.BlockSpec((tm, LANES), pos),
        ],
        out_specs=[
            pl.BlockSpec((1, A_HEADS, tm, LANES), lambda bi, i: (bi, 0, i, 0)),
            pl.BlockSpec((1, tm, A_KV2), tok),
            pl.BlockSpec((1, tm, A_KV2), tok),
        ],
        out_shape=[
            jax.ShapeDtypeStruct((b, A_HEADS, l, LANES), BF16),
            jax.ShapeDtypeStruct((b, l, A_KV2), BF16),
            jax.ShapeDtypeStruct((b, l, A_KV2), BF16),
        ],
        compiler_params=_cparams(("parallel", "parallel")),
        name="qkv_proj",
    )(x, g, wq, wk2, wv2, gmat, gq, gk, cos, sin_a, sin_b)


def _rope_tables(l):
    pos = jnp.arange(l)
    freqs = ROPE_THETA ** (-jnp.arange(0, ROPE_HALF, 2, dtype=F32) / ROPE_HALF)
    ang_r = (pos // GRID_W).astype(F32)[:, None] * freqs[None, :]
    ang_c = (pos % GRID_W).astype(F32)[:, None] * freqs[None, :]
    emb = jnp.concatenate([ang_r, ang_r, ang_c, ang_c], axis=-1)
    cos = jnp.cos(emb)
    sin = jnp.sin(emb)
    first = (jnp.arange(A_HD) % ROPE_HALF) < (ROPE_HALF // 2)
    sin_a = jnp.where(first, -sin, 0.0)
    sin_b = jnp.where(first, 0.0, sin)
    tile2 = lambda a: jnp.concatenate([a, a], axis=-1)
    return tile2(cos), tile2(sin_a), tile2(sin_b)


def _attn_kernel(q_ref, k_ref, v_ref, o_ref, m_scr, acc_scr, *, tq, tkc, nkc):
    q = q_ref[0].reshape(A_GROUP * tq, LANES)
    m_scr[...] = jnp.full_like(m_scr, NEG_INF)
    acc_scr[...] = jnp.zeros_like(acc_scr)

    def scores(c):
        kk = k_ref[0, pl.ds(pl.multiple_of(c * tkc, tkc), tkc), :]
        return lax.dot_general(q, kk, (((1,), (1,)), ((), ())), preferred_element_type=F32)

    def softmax_pv(s, c):
        vv = v_ref[0, pl.ds(pl.multiple_of(c * tkc, tkc), tkc), :]
        blocks = [s[:, j * LANES:(j + 1) * LANES] for j in range(tkc // LANES)]
        blk_max = functools.reduce(jnp.maximum, blocks)
        m_prev = m_scr[...]
        m_new = jnp.maximum(m_prev, jnp.max(blk_max, axis=-1, keepdims=True))
        alpha = jnp.exp2(m_prev - m_new)
        p = jnp.concatenate([jnp.exp2(blk - m_new).astype(BF16) for blk in blocks], axis=1)
        acc_scr[...] = alpha * acc_scr[...] + jnp.dot(p, vv, preferred_element_type=F32)
        m_scr[...] = m_new

    s_cur = scores(0)
    for c in range(nkc):
        s_next = scores(c + 1) if c + 1 < nkc else None
        softmax_pv(s_cur, c)
        s_cur = s_next
    acc = acc_scr[...]
    out = acc / acc[:, A_HD:A_HD + 1]
    lane_q = lax.broadcasted_iota(jnp.int32, (tq, LANES), 1)
    for pair in range(A_GROUP // 2):
        even = out[(2 * pair) * tq:(2 * pair + 1) * tq]
        odd = out[(2 * pair + 1) * tq:(2 * pair + 2) * tq]
        both = jnp.where(lane_q < A_HD, even, pltpu.roll(odd, A_HD, 1))
        o_ref[0, :, pair * LANES:(pair + 1) * LANES] = both.astype(o_ref.dtype)


def _attention(q, k2, v2, tq, tkc):
    b, _, l, _ = q.shape
    nkc = l // tkc
    kern = functools.partial(_attn_kernel, tq=tq, tkc=tkc, nkc=nkc)
    kv_map = lambda bi, g, i: (bi, 0, g)
    return pl.pallas_call(
        kern,
        grid=(b, A_KV_HEADS, l // tq),
        in_specs=[
            pl.BlockSpec((1, A_GROUP, tq, LANES), lambda bi, g, i: (bi, g, i, 0)),
            pl.BlockSpec((1, l, LANES), kv_map),
            pl.BlockSpec((1, l, LANES), kv_map),
        ],
        out_specs=pl.BlockSpec((1, tq, A_GROUP * A_HD), lambda bi, g, i: (bi, i, g)),
        out_shape=jax.ShapeDtypeStruct((b, l, A_Q), BF16),
        scratch_shapes=[
            pltpu.VMEM((A_GROUP * tq, LANES), F32),
            pltpu.VMEM((A_GROUP * tq, LANES), F32),
        ],
        compiler_params=_cparams(("parallel", "parallel", "arbitrary")),
        name="attention",
    )(q, k2, v2)


TOKEN_TILE = 512
ATTN_TQ = 512
ATTN_TKC = 1024


def _prep_weights(norm_mix, norm_mlp, a_w_in, a_b_gate, a_g_head, a_w_out,
                  b_w_qkv, b_g_q, b_g_k, b_w_o, w_up, w_down):
    w_in = a_w_in[0]
    n_gate = 4 * M_HEADS
    wg = jnp.zeros((D_MODEL, LANES), F32).at[:, :n_gate].set(w_in[:, 2 * M_QK + 2 * M_V:])
    bg = jnp.zeros((1, LANES), F32).at[0, :n_gate].set(a_b_gate[0])
    w_qkv = b_w_qkv[0]
    wq = w_qkv[:, :A_Q]
    wk = w_qkv[:, A_Q:A_Q + A_KV_HEADS * A_HD].reshape(D_MODEL, A_KV_HEADS, A_HD)
    wv = w_qkv[:, A_Q + A_KV_HEADS * A_HD:].reshape(D_MODEL, A_KV_HEADS, A_HD)
    dup = lambda w: jnp.pad(w, ((0, 0), (0, 0), (0, LANES - A_HD))).reshape(D_MODEL, A_KV2)
    head_id = jnp.arange(LANES) // A_HD
    gmat = jnp.where(head_id[:, None] == head_id[None, :], 1.0 / A_HD, 0.0)
    q_gain = b_g_q[0] * (A_HD ** -0.5 * math.log2(math.e))
    return dict(
        norm_mix=norm_mix.reshape(2, 1, D_MODEL),
        norm_mlp=norm_mlp.reshape(2, 1, D_MODEL),
        wq=w_in[:, :M_QK].astype(BF16),
        wkt=w_in[:, M_QK:2 * M_QK].T.astype(BF16),
        wv=w_in[:, 2 * M_QK:2 * M_QK + M_V].astype(BF16),
        wo=w_in[:, 2 * M_QK + M_V:2 * M_QK + 2 * M_V].astype(BF16),
        wg=wg.astype(BF16),
        bg=bg,
        g_head=a_g_head[0].reshape(1, M_V),
        w_out=a_w_out[0].astype(BF16),
        wq_attn=wq.astype(BF16),
        wk2=dup(wk).astype(BF16),
        wv2=dup(wv).astype(BF16),
        gmat=gmat.astype(BF16),
        gq=jnp.tile(q_gain, 2).reshape(1, LANES),
        gk=jnp.tile(b_g_k[0], 2).reshape(1, LANES),
        w_o=b_w_o[0].astype(BF16),
        w_up=w_up.astype(BF16),
        w_down=w_down.astype(BF16),
    )


def _trunk(x, w, rope):
    b, l, _ = x.shape
    t = b * l
    tm = TOKEN_TILE
    x2 = x.reshape(t, D_MODEL)
    q, kt, v, o, gates = _mlstm_proj(x2, w["norm_mix"][0], w["wq"], w["wkt"], w["wv"], w["wo"], w["wg"],
                                     w["bg"], tm)
    hf, hb = _mlstm_scan(q.reshape(b, l, M_QK), kt, v.reshape(b, l, M_V), gates.reshape(b, l, LANES))
    x2 = _out_mlp(_mlstm_out_mlp_kernel, "mlstm_out_mlp", x2, [hf.reshape(t, M_V), hb.reshape(t, M_V), o],
                  [w["g_head"]], w["w_out"], w["norm_mlp"][0], w["w_up"][0], w["w_down"][0], tm)
    cos, sin_a, sin_b = rope
    qa, k2, v2 = _qkv_proj(x2.reshape(b, l, D_MODEL), w["norm_mix"][1], w["wq_attn"], w["wk2"], w["wv2"],
                           w["gmat"], w["gq"], w["gk"], cos, sin_a, sin_b, tm)
    att = _attention(qa, k2, v2, ATTN_TQ, ATTN_TKC)
    x2 = _out_mlp(_attn_out_mlp_kernel, "attn_out_mlp", x2, [att.reshape(t, A_Q)], [], w["w_o"],
                  w["norm_mlp"][1], w["w_up"][1], w["w_down"][1], tm)
    return x2.reshape(b, l, D_MODEL)


def kernel(x_prompt, x_sample, norm_mix, norm_mlp, a_w_in, a_b_gate, a_g_head, a_w_out,
           b_w_qkv, b_g_q, b_g_k, b_w_o, w_up, w_down):
    w = _prep_weights(norm_mix, norm_mlp, a_w_in, a_b_gate, a_g_head, a_w_out,
                      b_w_qkv, b_g_q, b_g_k, b_w_o, w_up, w_down)
    rope = _rope_tables(max(x_prompt.shape[1], x_sample.shape[1]))
    return (_trunk(x_prompt, w, rope), _trunk(x_sample, w, rope))
```

```python
import functools
import math

import jax
import jax.numpy as jnp
from jax import lax
from jax.experimental import pallas as pl
from jax.experimental.pallas import tpu as pltpu

F32 = jnp.float32
BF16 = jnp.bfloat16

D_MODEL = 1024
EPS = 1e-6
GRID_W = 64
M_HEADS = 4
M_DQK = 128
M_DV = 256
M_CHUNK = 128
M_QK = M_HEADS * M_DQK
M_V = M_HEADS * M_DV
A_HEADS = 16
A_KV_HEADS = 4
A_GROUP = 4
A_HD = 64
ROPE_THETA = 10000.0
ROPE_HALF = 32
D_FF = 4096

LANES = 128
VMEM_LIMIT = 56 * 1024 * 1024

NEG_INF = float("-inf")


def _cparams(sem):
    return pltpu.CompilerParams(dimension_semantics=sem, vmem_limit_bytes=VMEM_LIMIT)


def _rms_scale(x):
    return lax.rsqrt(jnp.mean(x * x, axis=-1, keepdims=True) + EPS)


def _mlstm_proj_kernel(x_ref, g_ref, wq_ref, wkt_ref, wv_ref, wo_ref, wg_ref, bg_ref,
                       q_ref, kt_ref, v_ref, o_ref, gate_ref):
    x = x_ref[...]
    h = (x * _rms_scale(x) * g_ref[...]).astype(BF16)
    q = jnp.dot(h, wq_ref[...], preferred_element_type=F32)
    q_ref[...] = (q * (M_DQK ** -0.5)).astype(BF16)
    kt = lax.dot_general(wkt_ref[...], h, (((1,), (1,)), ((), ())), preferred_element_type=F32)
    kt_ref[...] = kt.astype(BF16)
    v_ref[...] = jnp.dot(h, wv_ref[...], preferred_element_type=F32).astype(BF16)
    o_ref[...] = jnp.dot(h, wo_ref[...], preferred_element_type=F32)
    gate_ref[...] = jnp.dot(h, wg_ref[...], preferred_element_type=F32) + bg_ref[...]


def _mlstm_proj(x2, g, wq, wkt, wv, wo, wg, bg, tm):
    t = x2.shape[0]
    const = lambda i: (0, 0)
    row = lambda i: (i, 0)
    return pl.pallas_call(
        _mlstm_proj_kernel,
        grid=(t // tm,),
        in_specs=[
            pl.BlockSpec((tm, D_MODEL), row),
            pl.BlockSpec((1, D_MODEL), const),
            pl.BlockSpec((D_MODEL, M_QK), const),
            pl.BlockSpec((M_QK, D_MODEL), const),
            pl.BlockSpec((D_MODEL, M_V), const),
            pl.BlockSpec((D_MODEL, M_V), const),
            pl.BlockSpec((D_MODEL, LANES), const),
            pl.BlockSpec((1, LANES), const),
        ],
        out_specs=[
            pl.BlockSpec((tm, M_QK), row),
            pl.BlockSpec((M_QK, tm), lambda i: (0, i)),
            pl.BlockSpec((tm, M_V), row),
            pl.BlockSpec((tm, M_V), row),
            pl.BlockSpec((tm, LANES), row),
        ],
        out_shape=[
            jax.ShapeDtypeStruct((t, M_QK), BF16),
            jax.ShapeDtypeStruct((M_QK, t), BF16),
            jax.ShapeDtypeStruct((t, M_V), BF16),
            jax.ShapeDtypeStruct((t, M_V), F32),
            jax.ShapeDtypeStruct((t, LANES), F32),
        ],
        compiler_params=_cparams(("parallel",)),
        name="mlstm_proj",
    )(x2, g, wq, wkt, wv, wo, wg, bg)


def _split3_dot(a, b_bf16, dims):
    hi = a.astype(BF16)
    r1 = a - hi.astype(F32)
    mid = r1.astype(BF16)
    lo = (r1 - mid.astype(F32)).astype(BF16)
    dn = (dims, ((), ()))
    out = lax.dot_general(hi, b_bf16, dn, preferred_element_type=F32)
    out = out + lax.dot_general(mid, b_bf16, dn, preferred_element_type=F32)
    out = out + lax.dot_general(lo, b_bf16, dn, preferred_element_type=F32)
    return out


N_GATE_ROWS = 4 * M_HEADS


def _log_sigmoid(x):
    return jnp.minimum(x, 0.0) - jnp.log1p(jnp.exp(-jnp.abs(x)))


def _mlstm_scan_kernel(qf_ref, ktf_ref, vf_ref, gf_ref, qb_ref, ktb_ref, vb_ref, gb_ref,
                       hf_ref, hb_ref, c_scr, m_scr):
    c = pl.program_id(1)

    @pl.when(c == 0)
    def _():
        c_scr[...] = jnp.zeros_like(c_scr)
        m_scr[...] = jnp.full_like(m_scr, NEG_INF)

    row_i = lax.broadcasted_iota(jnp.int32, (M_CHUNK, M_CHUNK), 0)
    col_i = lax.broadcasted_iota(jnp.int32, (M_CHUNK, M_CHUNK), 1)
    lower = row_i >= col_i
    upper = row_i <= col_i
    lane_i = lax.broadcasted_iota(jnp.int32, (M_CHUNK, LANES), 1)
    ones_col = jnp.where(lane_i == 0, 1.0, 0.0).astype(BF16)

    dirs = (
        (0, qf_ref, ktf_ref, vf_ref, gf_ref, hf_ref, lower),
        (1, qb_ref, ktb_ref, vb_ref, gb_ref, hb_ref, upper),
    )
    units = [dict(d=d, hd=hd, u=d * M_HEADS + hd, q_ref=t[1], kt_ref=t[2], v_ref=t[3], h_ref=t[5], mask=t[6])
             for d, t in enumerate(dirs) for hd in range(M_HEADS)]

    for d, _, _, _, g_ref, _, mask in dirs:
        tri = jnp.where(mask, 1.0, 0.0).astype(BF16)
        gates = g_ref[...]
        ls = _log_sigmoid(gates)
        gates_t = gates.T[:N_GATE_ROWS]
        ls_t = ls.T[:N_GATE_ROWS]
        cum_row = _split3_dot(ls_t, tri, ((1,), (1,)))
        cum_c = jnp.concatenate([cum_row, jnp.zeros((M_CHUNK - N_GATE_ROWS, M_CHUNK), F32)], axis=0).T
        for un in units[d * M_HEADS:(d + 1) * M_HEADS]:
            li_idx = 8 * d + un["hd"]
            lf_idx = li_idx + M_HEADS
            un["a_row"] = gates_t[li_idx:li_idx + 1, :] - cum_row[lf_idx:lf_idx + 1, :]
            un["b_col"] = cum_c[:, lf_idx:lf_idx + 1]
            un["b_last"] = jnp.sum(ls_t[lf_idx:lf_idx + 1, :], axis=-1, keepdims=True)

        for un in units[d * M_HEADS:(d + 1) * M_HEADS]:
            hd = un["hd"]
            q = un["q_ref"][0, :, hd * M_DQK:(hd + 1) * M_DQK]
            kt = un["kt_ref"][hd * M_DQK:(hd + 1) * M_DQK, :]
            v = un["v_ref"][0, :, hd * M_DV:(hd + 1) * M_DV]
            v_ext = jnp.concatenate([v, ones_col], axis=1)
            c_old = c_scr[un["u"]]
            m_old = m_scr[un["u"]]
            un["v_ext"], un["m_old"] = v_ext, m_old
            un["s_qk"] = jnp.dot(q, kt, preferred_element_type=F32)
            un["q_c"] = jnp.dot(q, c_old.astype(BF16), preferred_element_type=F32)

            a_row, b_last = un["a_row"], un["b_last"]
            m_new = b_last + jnp.maximum(m_old, jnp.max(a_row, axis=-1, keepdims=True))
            decay = jnp.exp(b_last + m_old - m_new)
            wk_row = jnp.exp(b_last + a_row - m_new)
            kw_t = (kt.astype(F32) * wk_row).astype(BF16)
            c_scr[un["u"]] = decay * c_old + jnp.dot(kw_t, v_ext, preferred_element_type=F32)
            m_scr[un["u"]] = m_new

    for un in units:
        a_masked = jnp.where(un["mask"], un["a_row"], NEG_INF)
        e_col = jnp.maximum(un["m_old"], jnp.max(a_masked, axis=-1, keepdims=True))
        un["m_row"] = un["b_col"] + e_col
        un["w_inter"] = jnp.exp(un["m_old"] - e_col)
        un["p"] = (jnp.exp(a_masked - e_col) * un["s_qk"]).astype(BF16)

    for un in units:
        hd = un["hd"]
        num = un["w_inter"] * un["q_c"] + jnp.dot(un["p"], un["v_ext"], preferred_element_type=F32)
        den = num[:, M_DV:M_DV + 1]
        h = num[:, :M_DV] / jnp.maximum(jnp.abs(den), jnp.exp(-un["m_row"]))
        un["h_ref"][0, :, hd * M_DV:(hd + 1) * M_DV] = h


def _mlstm_scan(q, kt, v, gates):
    b, l, _ = q.shape
    nc = l // M_CHUNK
    fwd = lambda bi, ci: (bi, ci, 0)
    bwd = lambda bi, ci: (bi, nc - 1 - ci, 0)
    specs = []
    for im in (fwd, bwd):
        kt_map = lambda bi, ci, im=im: (0, bi * nc + im(bi, ci)[1])
        specs += [
            pl.BlockSpec((1, M_CHUNK, M_QK), im),
            pl.BlockSpec((M_QK, M_CHUNK), kt_map),
            pl.BlockSpec((1, M_CHUNK, M_V), im),
            pl.BlockSpec((None, M_CHUNK, LANES), im),
        ]
    return pl.pallas_call(
        _mlstm_scan_kernel,
        grid=(b, nc),
        in_specs=specs,
        out_specs=[pl.BlockSpec((1, M_CHUNK, M_V), fwd), pl.BlockSpec((1, M_CHUNK, M_V), bwd)],
        out_shape=[jax.ShapeDtypeStruct((b, l, M_V), F32), jax.ShapeDtypeStruct((b, l, M_V), F32)],
        scratch_shapes=[
            pltpu.VMEM((2 * M_HEADS, M_DQK, M_DV + LANES), F32),
            pltpu.VMEM((2 * M_HEADS, 1, 1), F32),
        ],
        compiler_params=_cparams(("parallel", "arbitrary")),
        name="mlstm_scan",
    )(q, kt, v, gates, q, kt, v, gates)


FF_CHUNK = 1024


def _mlp_residual(x, g_ref, wup_ref, wdn_ref):
    h = (x * _rms_scale(x) * g_ref[...]).astype(BF16)
    acc = x
    for f in range(D_FF // FF_CHUNK):
        u = jnp.dot(h, wup_ref[:, f * FF_CHUNK:(f + 1) * FF_CHUNK], preferred_element_type=F32)
        u = jnp.maximum(u, 0.0)
        u = (u * u).astype(BF16)
        acc = acc + jnp.dot(u, wdn_ref[f * FF_CHUNK:(f + 1) * FF_CHUNK, :], preferred_element_type=F32)
    return acc


def _mlstm_out_mlp_kernel(x_ref, hf_ref, hb_ref, o_ref, gh_ref, w_ref, g_ref, wup_ref, wdn_ref, y_ref):
    hs = hf_ref[...] + hb_ref[...]
    parts = []
    for hd in range(M_HEADS):
        blk = hs[:, hd * M_DV:(hd + 1) * M_DV]
        parts.append(blk * _rms_scale(blk))
    hn = jnp.concatenate(parts, axis=1) * gh_ref[...]
    y = (jax.nn.sigmoid(o_ref[...]) * hn).astype(BF16)
    x1 = x_ref[...] + jnp.dot(y, w_ref[...], preferred_element_type=F32)
    y_ref[...] = _mlp_residual(x1, g_ref, wup_ref, wdn_ref)


def _attn_out_mlp_kernel(x_ref, a_ref, w_ref, g_ref, wup_ref, wdn_ref, y_ref):
    x1 = x_ref[...] + jnp.dot(a_ref[...], w_ref[...], preferred_element_type=F32)
    y_ref[...] = _mlp_residual(x1, g_ref, wup_ref, wdn_ref)


def _out_mlp(body, name, x2, acts, consts, w_out, g, wup, wdn, tm):
    t = x2.shape[0]
    const = lambda i: (0, 0)
    row = lambda i: (i, 0)
    once = dict(pipeline_mode=pl.Buffered(1))
    return pl.pallas_call(
        body,
        grid=(t // tm,),
        in_specs=(
            [pl.BlockSpec((tm, D_MODEL), row)]
            + [pl.BlockSpec((tm, a.shape[1]), row) for a in acts]
            + [pl.BlockSpec(c.shape, const, **once) for c in consts]
            + [
                pl.BlockSpec(w_out.shape, const, **once),
                pl.BlockSpec((1, D_MODEL), const, **once),
                pl.BlockSpec((D_MODEL, D_FF), const, **once),
                pl.BlockSpec((D_FF, D_MODEL), const, **once),
            ]
        ),
        out_specs=pl.BlockSpec((tm, D_MODEL), row),
        out_shape=jax.ShapeDtypeStruct((t, D_MODEL), F32),
        compiler_params=_cparams(("parallel",)),
        name=name,
    )(x2, *acts, *consts, w_out, g, wup, wdn)


A_Q = A_HEADS * A_HD
A_KV2 = A_KV_HEADS * LANES


def _headnorm_rope(x, gmat, gain, cos, sin_a, sin_b):
    ms = jnp.dot((x * x).astype(BF16), gmat, preferred_element_type=F32)
    y = x * lax.rsqrt(ms + EPS) * gain
    return y * cos + pltpu.roll(y, LANES - 16, 1) * sin_a + pltpu.roll(y, 16, 1) * sin_b


def _qkv_proj_kernel(x_ref, g_ref, wq_ref, wk_ref, wv_ref, gmat_ref, gq_ref, gk_ref,
                     cos_ref, sina_ref, sinb_ref, q_ref, k_ref, v_ref):
    x = x_ref[0]
    h = (x * _rms_scale(x) * g_ref[...]).astype(BF16)
    gmat = gmat_ref[...]
    cos, sin_a, sin_b = cos_ref[...], sina_ref[...], sinb_ref[...]
    q = jnp.dot(h, wq_ref[...], preferred_element_type=F32)
    for j in range(A_Q // LANES):
        blk = _headnorm_rope(q[:, j * LANES:(j + 1) * LANES], gmat, gq_ref[...], cos, sin_a, sin_b)
        q_ref[0, 2 * j] = blk.astype(BF16)
        q_ref[0, 2 * j + 1] = pltpu.roll(blk, A_HD, 1).astype(BF16)
    k = jnp.dot(h, wk_ref[...], preferred_element_type=F32)
    for j in range(A_KV2 // LANES):
        blk = _headnorm_rope(k[:, j * LANES:(j + 1) * LANES], gmat, gk_ref[...], cos, sin_a, sin_b)
        k_ref[0, :, j * LANES:(j + 1) * LANES] = blk.astype(BF16)
    v = jnp.dot(h, wv_ref[...], preferred_element_type=F32)
    lane = lax.broadcasted_iota(jnp.int32, v.shape, 1)
    v_ref[0] = jnp.where(lane % LANES == A_HD, 1.0, v).astype(BF16)


def _qkv_proj(x, g, wq, wk2, wv2, gmat, gq, gk, cos, sin_a, sin_b, tm):
    b, l, _ = x.shape
    const = lambda bi, i: (0, 0)
    tok = lambda bi, i: (bi, i, 0)
    pos = lambda bi, i: (i, 0)
    return pl.pallas_call(
        _qkv_proj_kernel,
        grid=(b, l // tm),
        in_specs=[
            pl.BlockSpec((1, tm, D_MODEL), tok),
            pl.BlockSpec((1, D_MODEL), const),
            pl.BlockSpec((D_MODEL, A_Q), const),
            pl.BlockSpec((D_MODEL, A_KV2), const),
            pl.BlockSpec((D_MODEL, A_KV2), const),
            pl.BlockSpec((LANES, LANES), const),
            pl.BlockSpec((1, LANES), const),
            pl.BlockSpec((1, LANES), const),
            pl.BlockSpec((tm, LANES), pos),
            pl.BlockSpec((tm, LANES), pos),
            pl.BlockSpec((tm, LANES), pos),
        ],
        out_specs=[
            pl.BlockSpec((1, A_HEADS, tm, LANES), lambda bi, i: (bi, 0, i, 0)),
            pl.BlockSpec((1, tm, A_KV2), tok),
            pl.BlockSpec((1, tm, A_KV2), tok),
        ],
        out_shape=[
            jax.ShapeDtypeStruct((b, A_HEADS, l, LANES), BF16),
            jax.ShapeDtypeStruct((b, l, A_KV2), BF16),
            jax.ShapeDtypeStruct((b, l, A_KV2), BF16),
        ],
        compiler_params=_cparams(("parallel", "parallel")),
        name="qkv_proj",
    )(x, g, wq, wk2, wv2, gmat, gq, gk, cos, sin_a, sin_b)


def _rope_tables(l):
    pos = jnp.arange(l)
    freqs = ROPE_THETA ** (-jnp.arange(0, ROPE_HALF, 2, dtype=F32) / ROPE_HALF)
    ang_r = (pos // GRID_W).astype(F32)[:, None] * freqs[None, :]
    ang_c = (pos % GRID_W).astype(F32)[:, None] * freqs[None, :]
    emb = jnp.concatenate([ang_r, ang_r, ang_c, ang_c], axis=-1)
    cos = jnp.cos(emb)
    sin = jnp.sin(emb)
    first = (jnp.arange(A_HD) % ROPE_HALF) < (ROPE_HALF // 2)
    sin_a = jnp.where(first, -sin, 0.0)
    sin_b = jnp.where(first, 0.0, sin)
    tile2 = lambda a: jnp.concatenate([a, a], axis=-1)
    return tile2(cos), tile2(sin_a), tile2(sin_b)


def _attn_kernel(q_ref, k_ref, v_ref, o_ref, m_scr, acc_scr, *, tq, tkc, nkc):
    q = q_ref[0].reshape(A_GROUP * tq, LANES)
    m_scr[...] = jnp.full_like(m_scr, NEG_INF)
    acc_scr[...] = jnp.zeros_like(acc_scr)

    def scores(c):
        kk = k_ref[0, pl.ds(pl.multiple_of(c * tkc, tkc), tkc), :]
        return lax.dot_general(q, kk, (((1,), (1,)), ((), ())), preferred_element_type=F32)

    def softmax_pv(s, c):
        vv = v_ref[0, pl.ds(pl.multiple_of(c * tkc, tkc), tkc), :]
        blocks = [s[:, j * LANES:(j + 1) * LANES] for j in range(tkc // LANES)]
        blk_max = functools.reduce(jnp.maximum, blocks)
        m_prev = m_scr[...]
        m_new = jnp.maximum(m_prev, jnp.max(blk_max, axis=-1, keepdims=True))
        alpha = jnp.exp2(m_prev - m_new)
        p = jnp.concatenate([jnp.exp2(blk - m_new).astype(BF16) for blk in blocks], axis=1)
        acc_scr[...] = alpha * acc_scr[...] + jnp.dot(p, vv, preferred_element_type=F32)
        m_scr[...] = m_new

    s_cur = scores(0)
    for c in range(nkc):
        s_next = scores(c + 1) if c + 1 < nkc else None
        softmax_pv(s_cur, c)
        s_cur = s_next
    acc = acc_scr[...]
    out = acc / acc[:, A_HD:A_HD + 1]
    lane_q = lax.broadcasted_iota(jnp.int32, (tq, LANES), 1)
    for pair in range(A_GROUP // 2):
        even = out[(2 * pair) * tq:(2 * pair + 1) * tq]
        odd = out[(2 * pair + 1) * tq:(2 * pair + 2) * tq]
        both = jnp.where(lane_q < A_HD, even, pltpu.roll(odd, A_HD, 1))
        o_ref[0, :, pair * LANES:(pair + 1) * LANES] = both.astype(o_ref.dtype)


def _attention(q, k2, v2, tq, tkc):
    b, _, l, _ = q.shape
    nkc = l // tkc
    kern = functools.partial(_attn_kernel, tq=tq, tkc=tkc, nkc=nkc)
    kv_map = lambda bi, g, i: (bi, 0, g)
    return pl.pallas_call(
        kern,
        grid=(b, A_KV_HEADS, l // tq),
        in_specs=[
            pl.BlockSpec((1, A_GROUP, tq, LANES), lambda bi, g, i: (bi, g, i, 0)),
            pl.BlockSpec((1, l, LANES), kv_map),
            pl.BlockSpec((1, l, LANES), kv_map),
        ],
        out_specs=pl.BlockSpec((1, tq, A_GROUP * A_HD), lambda bi, g, i: (bi, i, g)),
        out_shape=jax.ShapeDtypeStruct((b, l, A_Q), BF16),
        scratch_shapes=[
            pltpu.VMEM((A_GROUP * tq, LANES), F32),
            pltpu.VMEM((A_GROUP * tq, LANES), F32),
        ],
        compiler_params=_cparams(("parallel", "parallel", "arbitrary")),
        name="attention",
    )(q, k2, v2)


TOKEN_TILE = 512
ATTN_TQ = 256
ATTN_LONG_SEQ = 8192
ATTN_TKC = 1024


def _prep_weights(norm_mix, norm_mlp, a_w_in, a_b_gate, a_g_head, a_w_out,
                  b_w_qkv, b_g_q, b_g_k, b_w_o, w_up, w_down):
    w_in = a_w_in[0]
    n_gate = 4 * M_HEADS
    wg = jnp.zeros((D_MODEL, LANES), F32).at[:, :n_gate].set(w_in[:, 2 * M_QK + 2 * M_V:])
    bg = jnp.zeros((1, LANES), F32).at[0, :n_gate].set(a_b_gate[0])
    w_qkv = b_w_qkv[0]
    wq = w_qkv[:, :A_Q]
    wk = w_qkv[:, A_Q:A_Q + A_KV_HEADS * A_HD].reshape(D_MODEL, A_KV_HEADS, A_HD)
    wv = w_qkv[:, A_Q + A_KV_HEADS * A_HD:].reshape(D_MODEL, A_KV_HEADS, A_HD)
    dup = lambda w: jnp.pad(w, ((0, 0), (0, 0), (0, LANES - A_HD))).reshape(D_MODEL, A_KV2)
    head_id = jnp.arange(LANES) // A_HD
    gmat = jnp.where(head_id[:, None] == head_id[None, :], 1.0 / A_HD, 0.0)
    q_gain = b_g_q[0] * (A_HD ** -0.5 * math.log2(math.e))
    return dict(
        norm_mix=norm_mix.reshape(2, 1, D_MODEL),
        norm_mlp=norm_mlp.reshape(2, 1, D_MODEL),
        wq=w_in[:, :M_QK].astype(BF16),
        wkt=w_in[:, M_QK:2 * M_QK].T.astype(BF16),
        wv=w_in[:, 2 * M_QK:2 * M_QK + M_V].astype(BF16),
        wo=w_in[:, 2 * M_QK + M_V:2 * M_QK + 2 * M_V].astype(BF16),
        wg=wg.astype(BF16),
        bg=bg,
        g_head=a_g_head[0].reshape(1, M_V),
        w_out=a_w_out[0].astype(BF16),
        wq_attn=wq.astype(BF16),
        wk2=dup(wk).astype(BF16),
        wv2=dup(wv).astype(BF16),
        gmat=gmat.astype(BF16),
        gq=jnp.tile(q_gain, 2).reshape(1, LANES),
        gk=jnp.tile(b_g_k[0], 2).reshape(1, LANES),
        w_o=b_w_o[0].astype(BF16),
        w_up=w_up.astype(BF16),
        w_down=w_down.astype(BF16),
    )


def _trunk(x, w, rope):
    b, l, _ = x.shape
    t = b * l
    tm = TOKEN_TILE
    x2 = x.reshape(t, D_MODEL)
    q, kt, v, o, gates = _mlstm_proj(x2, w["norm_mix"][0], w["wq"], w["wkt"], w["wv"], w["wo"], w["wg"],
                                     w["bg"], tm)
    hf, hb = _mlstm_scan(q.reshape(b, l, M_QK), kt, v.reshape(b, l, M_V), gates.reshape(b, l, LANES))
    x2 = _out_mlp(_mlstm_out_mlp_kernel, "mlstm_out_mlp", x2, [hf.reshape(t, M_V), hb.reshape(t, M_V), o],
                  [w["g_head"]], w["w_out"], w["norm_mlp"][0], w["w_up"][0], w["w_down"][0], tm)
    cos, sin_a, sin_b = rope
    qa, k2, v2 = _qkv_proj(x2.reshape(b, l, D_MODEL), w["norm_mix"][1], w["wq_attn"], w["wk2"], w["wv2"],
                           w["gmat"], w["gq"], w["gk"], cos, sin_a, sin_b, tm)
    att = _attention(qa, k2, v2, ATTN_TQ * 2 if l <= ATTN_LONG_SEQ else ATTN_TQ, ATTN_TKC)
    x2 = _out_mlp(_attn_out_mlp_kernel, "attn_out_mlp", x2, [att.reshape(t, A_Q)], [], w["w_o"],
                  w["norm_mlp"][1], w["w_up"][1], w["w_down"][1], tm)
    return x2.reshape(b, l, D_MODEL)


def kernel(x_prompt, x_sample, norm_mix, norm_mlp, a_w_in, a_b_gate, a_g_head, a_w_out,
           b_w_qkv, b_g_q, b_g_k, b_w_o, w_up, w_down):
    w = _prep_weights(norm_mix, norm_mlp, a_w_in, a_b_gate, a_g_head, a_w_out,
                      b_w_qkv, b_g_q, b_g_k, b_w_o, w_up, w_down)
    rope = _rope_tables(max(x_prompt.shape[1], x_sample.shape[1]))
    return (_trunk(x_prompt, w, rope), _trunk(x_sample, w, rope))
```
